```python
import math
import jax, jax.numpy as jnp
from jax import lax
import numpy as np


D_MODEL = 2048
BATCH = 2
SEQ = 8192
DEPTH = 4
DEC_BATCH = 2
DEC_SEQ = 4096
PAST_LEN = 128

GRID_W = 64
BRANCH_WIDTH = 512
N_BRANCH = 4
POOL_WINDOWS = (2, 4, 8, 16)
POOL_GROUP = BRANCH_WIDTH // len(POOL_WINDOWS)
FOURIER_GROUPS = 4
FOURIER_GROUP = BRANCH_WIDTH // FOURIER_GROUPS
RET_HEADS = 4
RET_HEAD_DIM = BRANCH_WIDTH // RET_HEADS
RET_CHUNK = 128
ROPE_BASE = 10000.0
NA_HEADS = 8
NA_HEAD_DIM = BRANCH_WIDTH // NA_HEADS
NA_ROWS_MAX = 8
NA_COLS = 16
D_FF = 5632
EPS = 1e-6
NEG_INF = -1e30
OFF_POOL = 0
OFF_FOURIER = OFF_POOL + BRANCH_WIDTH
OFF_RET = OFF_FOURIER + BRANCH_WIDTH
OFF_NA = OFF_RET + 4 * BRANCH_WIDTH
IN_WIDTH = OFF_NA + 3 * BRANCH_WIDTH

kernel_name = 'hybrid_gated_bidir_encoder'


def rmsnorm(x, g):
    xf = x.astype(jnp.float32)
    y = xf * lax.rsqrt(jnp.mean(xf * xf, axis=-1, keepdims=True) + EPS)
    return (y * g.astype(jnp.float32)).astype(x.dtype)


def head_groupnorm(o):
    mu = jnp.mean(o, axis=-1, keepdims=True)
    var = jnp.mean(jnp.square(o - mu), axis=-1, keepdims=True)
    return (o - mu) * lax.rsqrt(var + EPS)


def rope(x):
    T = x.shape[1]
    half = x.shape[-1] // 2
    inv = ROPE_BASE ** (-jnp.arange(half, dtype=jnp.float32) / half)
    ang = jnp.arange(T, dtype=jnp.float32)[:, None] * inv[None, :]
    cos = jnp.cos(ang)[None, :, None, :]
    sin = jnp.sin(ang)[None, :, None, :]
    x1, x2 = x[..., :half], x[..., half:]
    return jnp.concatenate([x1 * cos - x2 * sin, x1 * sin + x2 * cos], axis=-1)


def centred_pool_minus_self(x, window):
    B, T, C = x.shape
    S = jnp.concatenate([jnp.zeros((B, 1, C), jnp.float32), jnp.cumsum(x, axis=1)], axis=1)
    t = jnp.arange(T)
    lo = jnp.clip(t - window // 2, 0, T)
    hi = jnp.clip(t + window // 2, 0, T)
    count = (hi - lo).astype(jnp.float32)
    return (S[:, hi] - S[:, lo]) / count[None, :, None] - x


def retention_one_direction(q, k, v, log_gamma, strict):
    B, T, H, d = q.shape
    C = RET_CHUNK
    N = T // C
    qc = q.reshape(B, N, C, H, d)
    kc = k.reshape(B, N, C, H, d)
    vc = v.reshape(B, N, C, H, d)
    i = jnp.arange(C, dtype=jnp.float32)
    diff = i[:, None] - i[None, :]
    mask = diff > 0 if strict else diff >= 0
    dmat = jnp.where(mask[None], jnp.exp(log_gamma[:, None, None] * jnp.maximum(diff, 0.0)[None]), 0.0)
    s = jnp.einsum('bnihd,bnjhd->bnhij', qc, kc) * dmat[None, None]
    inner = jnp.einsum('bnhij,bnjhe->bnihe', s, vc)
    k_dec = kc * jnp.exp((C - 1 - i)[:, None] * log_gamma[None, :])[None, None, :, :, None]
    kv = jnp.einsum('bnjhd,bnjhe->nbhde', k_dec, vc)
    chunk_decay = jnp.exp(log_gamma * C)[None, :, None, None]

    def step(R, kv_n):
        return chunk_decay * R + kv_n, R

    _, R_prev = lax.scan(step, jnp.zeros((B, H, d, d), jnp.float32), kv)
    q_dec = qc * jnp.exp((i + 1.0)[:, None] * log_gamma[None, :])[None, None, :, :, None]
    cross = jnp.einsum('bnihd,nbhde->bnihe', q_dec, R_prev)
    return (inner + cross).reshape(B, T, H, d)


def bidir_retention(q, k, v, log_gamma_f, log_gamma_b):
    fwd = retention_one_direction(q, k, v, log_gamma_f, False)
    bwd = retention_one_direction(q[:, ::-1], k[:, ::-1], v[:, ::-1], log_gamma_b, True)[:, ::-1]
    return fwd + bwd


def neighbourhood_attention(q, k, v, rpb):
    B, T, H, dh = q.shape
    rows = T // GRID_W
    kr = min(NA_ROWS_MAX, rows)
    r = jnp.arange(rows)
    col = jnp.arange(GRID_W)
    rs = jnp.clip(r - kr // 2, 0, rows - kr)
    idx_r = rs[:, None] + jnp.arange(kr)[None, :]
    cs = jnp.clip(col - NA_COLS // 2, 0, GRID_W - NA_COLS)
    in_win = (col[None, :] >= cs[:, None]) & (col[None, :] < cs[:, None] + NA_COLS)
    dr = idx_r - r[:, None] + (NA_ROWS_MAX - 1)
    dc = jnp.clip(col[None, :] - col[:, None], -(NA_COLS - 1), NA_COLS - 1) + (NA_COLS - 1)
    bias = rpb[:, dr[:, None, :, None], dc[None, :, None, :]]
    qg = q.reshape(B, rows, GRID_W, H, dh) * (dh ** -0.5)
    kg = k.reshape(B, rows, GRID_W, H, dh)[:, idx_r]
    vg = v.reshape(B, rows, GRID_W, H, dh)[:, idx_r]
    s = jnp.einsum('brqhd,brimhd->bhrqim', qg, kg).astype(jnp.float32) + bias.astype(jnp.float32)[None]
    s = jnp.where(in_win[:, None, :], s, NEG_INF)
    p = jax.nn.softmax(s.reshape(B, H, rows, GRID_W, kr * GRID_W), axis=-1).reshape(s.shape).astype(v.dtype)
    o = jnp.einsum('bhrqim,brimhd->brqhd', p, vg)
    return o.reshape(B, T, H * dh)


def token_mixer(h, w_in, pool_w, pool_scale, decay_f, decay_b, rpb, w_branch, w_gate, b_gate, w_out):
    B, T, _ = h.shape
    W = BRANCH_WIDTH
    p = h @ w_in
    a = p[..., OFF_POOL:OFF_POOL + W].astype(jnp.float32)
    pooled = jnp.concatenate(
        [centred_pool_minus_self(a[..., gi * POOL_GROUP:(gi + 1) * POOL_GROUP], w) for gi, w in enumerate(POOL_WINDOWS)],
        axis=-1).astype(h.dtype).reshape(B, T, len(POOL_WINDOWS), POOL_GROUP)
    br_pool = jnp.einsum('btgc,gce->btge', pooled, pool_w).reshape(B, T, W) * pool_scale
    f = p[..., OFF_FOURIER:OFF_FOURIER + W].astype(jnp.float32).reshape(B, T, FOURIER_GROUPS, FOURIER_GROUP)
    br_fourier = jnp.fft.fftn(f, axes=(1, 3), norm='ortho').real.reshape(B, T, W).astype(h.dtype)
    rr = p[..., OFF_RET:OFF_RET + 4 * W].astype(jnp.float32).reshape(B, T, 4, RET_HEADS, RET_HEAD_DIM)
    q = rope(rr[:, :, 0])
    k = rope(rr[:, :, 1]) * (RET_HEAD_DIM ** -0.5)
    v = rr[:, :, 2]
    g = rr[:, :, 3].reshape(B, T, W)
    o = bidir_retention(q, k, v, jax.nn.log_sigmoid(decay_f.astype(jnp.float32)),
                        jax.nn.log_sigmoid(decay_b.astype(jnp.float32)))
    br_ret = (head_groupnorm(o).reshape(B, T, W) * jax.nn.silu(g)).astype(h.dtype)
    nq = p[..., OFF_NA:OFF_NA + 3 * W].reshape(B, T, 3, NA_HEADS, NA_HEAD_DIM)
    br_na = neighbourhood_attention(nq[:, :, 0], nq[:, :, 1], nq[:, :, 2], rpb)
    gates = jax.nn.sigmoid((h @ w_gate + b_gate).astype(jnp.float32)).astype(h.dtype).reshape(B, T, N_BRANCH, D_MODEL)
    branches = (br_pool, br_fourier, br_ret, br_na)
    merged = gates[:, :, 0] * (branches[0] @ w_branch[0])
    for bi in range(1, N_BRANCH):
        merged = merged + gates[:, :, bi] * (branches[bi] @ w_branch[bi])
    return merged @ w_out


def dwconv3(u, w, b):
    up = jnp.pad(u, ((0, 0), (1, 1), (0, 0)))
    return up[:, :-2] * w[0] + up[:, 1:-1] * w[1] + up[:, 2:] * w[2] + b


def conv_ffn(h, w_up, conv_w, conv_b, w_down):
    u = dwconv3(h @ w_up, conv_w, conv_b)
    a, b = jnp.split(u, 2, axis=-1)
    return (jax.nn.gelu(a, approximate=True) * b) @ w_down


def encoder_trunk(x, c, w_ada, b_ada, g_mix, w_in, pool_w, pool_scale, ret_decay_f, ret_decay_b, na_rpb,
                  w_branch, w_gate, b_gate, w_out, g_ffn, w_up, conv_w, conv_b, w_down, g_final):
    for l in range(DEPTH):
        mod = (jax.nn.silu(c) @ w_ada[l] + b_ada[l])[:, None, :]
        shift1, scale1, gate1, shift2, scale2, gate2 = jnp.split(mod, 6, axis=-1)
        h = rmsnorm(x, g_mix[l]) * (1.0 + scale1) + shift1
        x = x + gate1 * token_mixer(h, w_in[l], pool_w[l], pool_scale[l], ret_decay_f[l], ret_decay_b[l],
                                    na_rpb[l], w_branch[l], w_gate[l], b_gate[l], w_out[l])
        h = rmsnorm(x, g_ffn[l]) * (1.0 + scale2) + shift2
        x = x + gate2 * conv_ffn(h, w_up[l], conv_w[l], conv_b[l], w_down[l])
    return rmsnorm(x, g_final)


def setup_inputs(seed: int = 0) -> dict:
    key = jax.random.key(seed)
    ks = jax.random.split(key, 24)

    def nrm(k, shape, s):
        return jax.random.normal(k, shape, jnp.float32) * s

    g0 = 1.0 - 2.0 ** (-5.0 - jnp.arange(RET_HEADS, dtype=jnp.float32))
    decay_logit = jnp.log(g0) - jnp.log1p(-g0)
    return {
        'x_prompt': nrm(ks[0], (BATCH, SEQ, D_MODEL), 1.0),
        'x_sample': nrm(ks[1], (DEC_BATCH, DEC_SEQ, D_MODEL), 1.0),
        'c_prompt': nrm(ks[2], (BATCH, D_MODEL), 1.0),
        'c_sample': nrm(ks[3], (DEC_BATCH, D_MODEL), 1.0),
        'w_ada': nrm(ks[4], (DEPTH, D_MODEL, 6 * D_MODEL), 0.5 * D_MODEL ** -0.5),
        'b_ada': nrm(ks[5], (DEPTH, 6 * D_MODEL), 0.02),
        'g_mix': 1.0 + nrm(ks[6], (DEPTH, D_MODEL), 0.02),
        'w_in': nrm(ks[7], (DEPTH, D_MODEL, IN_WIDTH), D_MODEL ** -0.5),
        'pool_w': nrm(ks[8], (DEPTH, len(POOL_WINDOWS), POOL_GROUP, POOL_GROUP), POOL_GROUP ** -0.5),
        'pool_scale': 1.0 + nrm(ks[9], (DEPTH, BRANCH_WIDTH), 0.1),
        'ret_decay_f': decay_logit[None, :] + nrm(ks[10], (DEPTH, RET_HEADS), 0.05),
        'ret_decay_b': decay_logit[None, :] + nrm(ks[11], (DEPTH, RET_HEADS), 0.05),
        'na_rpb': nrm(ks[12], (DEPTH, NA_HEADS, 2 * NA_ROWS_MAX - 1, 2 * NA_COLS - 1), 0.02),
        'w_branch': nrm(ks[13], (DEPTH, N_BRANCH, BRANCH_WIDTH, D_MODEL), BRANCH_WIDTH ** -0.5),
        'w_gate': nrm(ks[14], (DEPTH, D_MODEL, N_BRANCH * D_MODEL), D_MODEL ** -0.5),
        'b_gate': nrm(ks[15], (DEPTH, N_BRANCH * D_MODEL), 0.02),
        'w_out': nrm(ks[16], (DEPTH, D_MODEL, D_MODEL), D_MODEL ** -0.5),
        'g_ffn': 1.0 + nrm(ks[17], (DEPTH, D_MODEL), 0.02),
        'w_up': nrm(ks[18], (DEPTH, D_MODEL, 2 * D_FF), D_MODEL ** -0.5),
        'conv_w': nrm(ks[19], (DEPTH, 3, 2 * D_FF), 3.0 ** -0.5),
        'conv_b': nrm(ks[20], (DEPTH, 2 * D_FF), 0.02),
        'w_down': nrm(ks[21], (DEPTH, D_FF, D_MODEL), D_FF ** -0.5),
        'g_final': 1.0 + nrm(ks[22], (D_MODEL,), 0.02),
    }


def reference(x_prompt, x_sample, c_prompt, c_sample, w_ada, b_ada, g_mix, w_in, pool_w, pool_scale,
              ret_decay_f, ret_decay_b, na_rpb, w_branch, w_gate, b_gate, w_out, g_ffn, w_up, conv_w,
              conv_b, w_down, g_final):
    y_prompt = encoder_trunk(x_prompt, c_prompt, w_ada, b_ada, g_mix, w_in, pool_w, pool_scale, ret_decay_f,
                             ret_decay_b, na_rpb, w_branch, w_gate, b_gate, w_out, g_ffn, w_up, conv_w,
                             conv_b, w_down, g_final)
    y_sample = encoder_trunk(x_sample, c_sample, w_ada, b_ada, g_mix, w_in, pool_w, pool_scale, ret_decay_f,
                             ret_decay_b, na_rpb, w_branch, w_gate, b_gate, w_out, g_ffn, w_up, conv_w,
                             conv_b, w_down, g_final)
    return (y_prompt, y_sample)
```

```python
import functools
import math

import numpy as np
import jax
import jax.numpy as jnp
from jax import lax
from jax.experimental import pallas as pl
from jax.experimental.pallas import tpu as pltpu

F32 = jnp.float32
BF16 = jnp.bfloat16

D_MODEL = 2048
DEPTH = 4
GRID_W = 64
BRANCH_WIDTH = 512
N_BRANCH = 4
POOL_WINDOWS = (2, 4, 8, 16)
POOL_GROUP = 128
FOURIER_GROUP = 128
RET_HEADS = 4
RET_HEAD_DIM = 128
ROPE_BASE = 10000.0
NA_HEADS = 8
NA_HEAD_DIM = 64
NA_ROWS_MAX = 8
NA_COLS = 16
D_FF = 5632
EPS = 1e-6
NEG_INF = -1e30
IN_WIDTH = 9 * BRANCH_WIDTH

VMEM_LIMIT_BYTES = 56 * 1024 * 1024
LANES = 128
BF16_SUBLANES = 16

TM = 512
TN_IN = 512
TN_MERGE = 256
TF = 512
HALO = BF16_SUBLANES
POOL_TILE = 512
POOL_HALO = 8
F_T2 = 128
F1_TILE = 8
F2_TILE = 8
RET_CHUNK = 256
NA_ROWS_PER_STEP = 8


def _params(sem):
    return pltpu.CompilerParams(dimension_semantics=sem, vmem_limit_bytes=VMEM_LIMIT_BYTES)


def _norm_mod(x, g, scale, shift):
    ms = jnp.mean(x * x, axis=-1, keepdims=True)
    return (x * lax.rsqrt(ms + EPS) * g) * (1.0 + scale) + shift


def _ada_kernel(c_ref, w_ref, b_ref, o_ref):
    c = c_ref[...]
    s = (c * jax.nn.sigmoid(c)).astype(BF16)
    o_ref[0] = jnp.dot(s, w_ref[0].astype(BF16), preferred_element_type=F32) + b_ref[0]


def _ada(c8, w_ada, b_ada):
    depth, d, n = w_ada.shape
    tn = 1024
    return pl.pallas_call(
        _ada_kernel,
        out_shape=jax.ShapeDtypeStruct((depth, 8, n), F32),
        grid=(depth, n // tn),
        in_specs=[
            pl.BlockSpec((8, d), lambda l, j: (0, 0)),
            pl.BlockSpec((1, d, tn), lambda l, j: (l, 0, j)),
            pl.BlockSpec((1, 1, tn), lambda l, j: (l, 0, j)),
        ],
        out_specs=pl.BlockSpec((1, 8, tn), lambda l, j: (l, 0, j)),
        compiler_params=_params(("arbitrary", "arbitrary")),
        name="ada",
    )(c8, w_ada, b_ada.reshape(depth, 1, n))


def _rope_tile(acc, cos, sin, scale):
    parts = []
    for hh in range(acc.shape[1] // RET_HEAD_DIM):
        ch = acc[:, hh * RET_HEAD_DIM:(hh + 1) * RET_HEAD_DIM]
        parts.append((ch * cos + pltpu.roll(ch, RET_HEAD_DIM // 2, 1) * sin) * scale)
    return jnp.concatenate(parts, axis=1)


def _in_kernel(x_ref, g_ref, sc_ref, sh_ref, cos_ref, sin_ref, w_ref,
               pool_ref, four_ref, ret_ref, na_ref, h_ref):
    j = pl.program_id(1)

    @pl.when(j == 0)
    def _():
        h_ref[...] = _norm_mod(x_ref[...], g_ref[...], sc_ref[0], sh_ref[0]).astype(BF16)

    acc = jnp.dot(h_ref[...], w_ref[...], preferred_element_type=F32)

    @pl.when(j == 0)
    def _():
        pool_ref[...] = acc

    @pl.when(j == 1)
    def _():
        four_ref[...] = acc.astype(BF16)

    @pl.when(j == 2)
    def _():
        ret_ref[...] = _rope_tile(acc, cos_ref[...], sin_ref[...], 1.0).astype(BF16)

    @pl.when(j == 3)
    def _():
        ret_ref[...] = _rope_tile(acc, cos_ref[...], sin_ref[...], RET_HEAD_DIM ** -0.5).astype(BF16)

    @pl.when((j == 4) | (j == 5))
    def _():
        ret_ref[...] = acc.astype(BF16)

    @pl.when(j == 6)
    def _():
        na_ref[...] = (acc * (NA_HEAD_DIM ** -0.5)).astype(BF16)

    @pl.when(j >= 7)
    def _():
        na_ref[...] = acc.astype(BF16)


def _in_proj(x2, g, mods, cos_t, sin_t, w_in, seq):
    rows, d = x2.shape
    tps = seq // TM
    nj = IN_WIDTH // TN_IN

    def mod_spec(k):
        return pl.BlockSpec((1, 1, d), lambda i, j: ((i // tps) * 6 + k, 0, 0))

    return pl.pallas_call(
        _in_kernel,
        out_shape=(
            jax.ShapeDtypeStruct((rows, BRANCH_WIDTH), F32),
            jax.ShapeDtypeStruct((rows, BRANCH_WIDTH), BF16),
            jax.ShapeDtypeStruct((rows, 4 * BRANCH_WIDTH), BF16),
            jax.ShapeDtypeStruct((rows, 3 * BRANCH_WIDTH), BF16),
        ),
        grid=(rows // TM, nj),
        in_specs=[
            pl.BlockSpec((TM, d), lambda i, j: (i, 0)),
            pl.BlockSpec((1, d), lambda i, j: (0, 0)),
            mod_spec(1),
            mod_spec(0),
            pl.BlockSpec((TM, RET_HEAD_DIM), lambda i, j: (i % tps, 0)),
            pl.BlockSpec((TM, RET_HEAD_DIM), lambda i, j: (i % tps, 0)),
            pl.BlockSpec((d, TN_IN), lambda i, j: (0, j)),
        ],
        out_specs=(
            pl.BlockSpec((TM, TN_IN), lambda i, j: (i, 0)),
            pl.BlockSpec((TM, TN_IN), lambda i, j: (i, 0)),
            pl.BlockSpec((TM, TN_IN), lambda i, j: (i, jnp.clip(j - 2, 0, 3))),
            pl.BlockSpec((TM, TN_IN), lambda i, j: (i, jnp.clip(j - 6, 0, 2))),
        ),
        scratch_shapes=[pltpu.VMEM((TM, d), BF16)],
        compiler_params=_params(("arbitrary", "arbitrary")),
        name="in_proj",
    )(x2, g, mods, mods, cos_t, sin_t, w_in)


def _pool_kernel(a_ref, ap_ref, an_ref, w_ref, s_ref, o_ref, ext_ref, *, tps, seq):
    tp = a_ref.shape[0]
    ti = pl.program_id(0) % tps
    ext_ref[0:POOL_HALO, :] = jnp.where(ti == 0, 0.0, ap_ref[...])
    ext_ref[POOL_HALO:POOL_HALO + tp, :] = a_ref[...]
    ext_ref[POOL_HALO + tp:2 * POOL_HALO + tp, :] = jnp.where(ti == tps - 1, 0.0, an_ref[...])
    t = ti * tp + lax.broadcasted_iota(jnp.int32, (tp, 1), 0)
    for gi, w in enumerate(POOL_WINDOWS):
        cols = slice(gi * POOL_GROUP, (gi + 1) * POOL_GROUP)
        half = w // 2
        acc = ext_ref[pl.ds(POOL_HALO - half, tp), cols]
        for k in range(-half + 1, half):
            acc = acc + ext_ref[pl.ds(POOL_HALO + k, tp), cols]
        count = (jnp.minimum(t + half, seq) - jnp.maximum(t - half, 0)).astype(F32)
        pooled = acc / count - a_ref[:, cols]
        y = jnp.dot(pooled.astype(BF16), w_ref[gi], preferred_element_type=F32) * s_ref[:, cols]
        o_ref[:, cols] = y.astype(BF16)


def _pool_branch(a, pool_w, pool_scale, seq):
    rows, w = a.shape
    tp = POOL_TILE
    tps = seq // tp
    hb = tp // POOL_HALO
    nhalo = rows // POOL_HALO
    return pl.pallas_call(
        functools.partial(_pool_kernel, tps=tps, seq=seq),
        out_shape=jax.ShapeDtypeStruct((rows, w), BF16),
        grid=(rows // tp,),
        in_specs=[
            pl.BlockSpec((tp, w), lambda i: (i, 0)),
            pl.BlockSpec((POOL_HALO, w), lambda i: (jnp.maximum(i * hb - 1, 0), 0)),
            pl.BlockSpec((POOL_HALO, w), lambda i: (jnp.minimum((i + 1) * hb, nhalo - 1), 0)),
            pl.BlockSpec((len(POOL_WINDOWS), POOL_GROUP, POOL_GROUP), lambda i: (0, 0, 0)),
            pl.BlockSpec((1, w), lambda i: (0, 0)),
        ],
        out_specs=pl.BlockSpec((tp, w), lambda i: (i, 0)),
        scratch_shapes=[pltpu.VMEM((tp + 2 * POOL_HALO, w), F32)],
        compiler_params=_params(("arbitrary",)),
        name="pool",
    )(a, a, a, pool_w, pool_scale)


def _f1_kernel(x_ref, m1_ref, tc_ref, ts_ref, yr_ref, yi_ref):
    n1 = x_ref.shape[1]
    y = jnp.dot(m1_ref[...], x_ref[0], preferred_element_type=F32)
    yr = y[:n1]
    yi = y[n1:]
    c = tc_ref[...]
    s = ts_ref[...]
    yr_ref[0] = (yr * c + yi * s).astype(BF16)
    yi_ref[0] = (yi * c - yr * s).astype(BF16)


def _f2_kernel(yr_ref, yi_ref, ma_ref, mb_ref, cc_ref, cs_ref, o_ref, zr_ref, zi_ref, *, scale):
    n2 = F_T2
    w = o_ref.shape[2] // F2_TILE
    for s in range(F2_TILE):
        rows = slice(s * n2, (s + 1) * n2)
        z = (jnp.dot(ma_ref[...], yr_ref[0, rows, :], preferred_element_type=F32)
             + jnp.dot(mb_ref[...], yi_ref[0, rows, :], preferred_element_type=F32))
        zr_ref[rows, :] = z[:n2].astype(BF16)
        zi_ref[rows, :] = z[n2:].astype(BF16)
    o = (jnp.dot(zr_ref[...], cc_ref[...], preferred_element_type=F32)
         + jnp.dot(zi_ref[...], cs_ref[...], preferred_element_type=F32)) * scale
    for s in range(F2_TILE):
        o_ref[0, :, s * w:(s + 1) * w] = o[s * n2:(s + 1) * n2].astype(BF16)


def _dft_cos_sin(n):
    k = np.arange(n)
    ang = 2.0 * np.pi * ((k[:, None] * k[None, :]) % n) / n
    return np.cos(ang), np.sin(ang)


def _fourier_consts(seq):
    n2 = F_T2
    n1 = seq // n2
    c1, s1 = _dft_cos_sin(n1)
    c2, s2 = _dft_cos_sin(n2)
    cg, sg = _dft_cos_sin(FOURIER_GROUP)
    eye = np.eye(BRANCH_WIDTH // FOURIER_GROUP)
    m1 = jnp.asarray(np.concatenate([c1, -s1], axis=0), BF16)
    ma = jnp.asarray(np.concatenate([c2, -s2], axis=0), BF16)
    mb = jnp.asarray(np.concatenate([s2, c2], axis=0), BF16)
    cc = jnp.asarray(np.kron(eye, cg), BF16)
    cs = jnp.asarray(np.kron(eye, sg), BF16)
    prod = (jnp.arange(n1, dtype=jnp.int32)[:, None] * jnp.arange(n2, dtype=jnp.int32)[None, :]) % seq
    ang = prod.astype(F32) * (2.0 * math.pi / seq)
    shape = (n1, n2, BRANCH_WIDTH)
    tc = jnp.broadcast_to(jnp.cos(ang)[:, :, None], shape).reshape(n1, n2 * BRANCH_WIDTH)
    ts = jnp.broadcast_to(jnp.sin(ang)[:, :, None], shape).reshape(n1, n2 * BRANCH_WIDTH)
    return dict(m1=m1, ma=ma, mb=mb, cc=cc, cs=cs, tc=tc, ts=ts)


def _fourier_branch(f, fc, batch, seq):
    w = BRANCH_WIDTH
    n2 = F_T2
    n1 = seq // n2
    blk = F1_TILE * w
    x3 = f.reshape(batch, n1, n2 * w)
    yr, yi = pl.pallas_call(
        _f1_kernel,
        out_shape=(jax.ShapeDtypeStruct((batch, n1, n2 * w), BF16),) * 2,
        grid=(n2 // F1_TILE, batch),
        in_specs=[
            pl.BlockSpec((1, n1, blk), lambda j, b: (b, 0, j)),
            pl.BlockSpec((2 * n1, n1), lambda j, b: (0, 0)),
            pl.BlockSpec((n1, blk), lambda j, b: (0, j)),
            pl.BlockSpec((n1, blk), lambda j, b: (0, j)),
        ],
        out_specs=(pl.BlockSpec((1, n1, blk), lambda j, b: (b, 0, j)),) * 2,
        compiler_params=_params(("arbitrary", "arbitrary")),
        name="fourier1",
    )(x3, fc["m1"], fc["tc"], fc["ts"])
    yr = yr.reshape(batch, n1 * n2, w)
    yi = yi.reshape(batch, n1 * n2, w)
    rows = F2_TILE * n2
    out = pl.pallas_call(
        functools.partial(_f2_kernel, scale=1.0 / math.sqrt(seq * FOURIER_GROUP)),
        out_shape=jax.ShapeDtypeStruct((batch, n2, n1 * w), BF16),
        grid=(batch, n1 // F2_TILE),
        in_specs=[
            pl.BlockSpec((1, rows, w), lambda b, j: (b, j, 0)),
            pl.BlockSpec((1, rows, w), lambda b, j: (b, j, 0)),
            pl.BlockSpec((2 * n2, n2), lambda b, j: (0, 0)),
            pl.BlockSpec((2 * n2, n2), lambda b, j: (0, 0)),
            pl.BlockSpec((w, w), lambda b, j: (0, 0)),
            pl.BlockSpec((w, w), lambda b, j: (0, 0)),
        ],
        out_specs=pl.BlockSpec((1, n2, F2_TILE * w), lambda b, j: (b, 0, j)),
        scratch_shapes=[pltpu.VMEM((rows, w), BF16), pltpu.VMEM((rows, w), BF16)],
        compiler_params=_params(("arbitrary", "arbitrary")),
        name="fourier2",
    )(yr, yi, fc["ma"], fc["mb"], fc["cc"], fc["cs"])
    return out.reshape(batch * seq, w)


def _log_sigmoid(x):
    return jnp.minimum(x, 0.0) - jnp.log1p(jnp.exp(-jnp.abs(x)))


def _ret_kernel(q_ref, k_ref, v_ref, g_ref, df_ref, db_ref, o_ref, kv_ref, r_ref, *, seq):
    c = RET_CHUNK
    d = RET_HEAD_DIM
    nc = seq // c
    lgf = _log_sigmoid(df_ref[0])
    lgb = _log_sigmoid(db_ref[0])
    ii = lax.broadcasted_iota(jnp.int32, (c, 1), 0).astype(F32)
    kdf = jnp.exp(lgf * (c - 1.0 - ii))
    kdb = jnp.exp(lgb * ii)
    qdf = jnp.exp(lgf * (ii + 1.0))
    qdb = jnp.exp(lgb * (c - ii))
    diff = (lax.broadcasted_iota(jnp.int32, (c, c), 0)
            - lax.broadcasted_iota(jnp.int32, (c, c), 1)).astype(F32)
    lgf_c = jnp.concatenate([lgf] * (c // d), axis=1)
    lgb_c = jnp.concatenate([lgb] * (c // d), axis=1)
    dmat = jnp.where(diff >= 0.0, jnp.exp(lgf_c * jnp.maximum(diff, 0.0)),
                     jnp.exp(lgb_c * jnp.maximum(-diff, 0.0)))
    chunk_f = jnp.exp(lgf * float(c))
    chunk_b = jnp.exp(lgb * float(c))

    def kv_body(n, carry):
        rows = pl.ds(pl.multiple_of(n * c, c), c)
        kc = k_ref[0, rows, :].astype(F32)
        kd = jnp.concatenate([kc * kdf, kc * kdb], axis=1).astype(BF16)
        kv_ref[n] = lax.dot_general(kd, v_ref[0, rows, :], (((0,), (0,)), ((), ())),
                                    preferred_element_type=F32)
        return carry

    lax.fori_loop(0, nc, kv_body, 0)

    def fwd_body(n, state):
        r_ref[n, 0:d, :] = state.astype(BF16)
        return chunk_f * state + kv_ref[n, 0:d, :]

    lax.fori_loop(0, nc, fwd_body, jnp.zeros((d, d), F32))

    def bwd_body(m, state):
        n = nc - 1 - m
        r_ref[n, d:2 * d, :] = state.astype(BF16)
        return chunk_b * state + kv_ref[n, d:2 * d, :]

    lax.fori_loop(0, nc, bwd_body, jnp.zeros((d, d), F32))

    def out_body(n, carry):
        rows = pl.ds(pl.multiple_of(n * c, c), c)
        qb = q_ref[0, rows, :]
        s = lax.dot_general(qb, k_ref[0, rows, :], (((1,), (1,)), ((), ())),
                            preferred_element_type=F32) * dmat
        inner = jnp.dot(s.astype(BF16), v_ref[0, rows, :], preferred_element_type=F32)
        qc = qb.astype(F32)
        qd = jnp.concatenate([qc * qdf, qc * qdb], axis=1).astype(BF16)
        o = inner + jnp.dot(qd, r_ref[n], preferred_element_type=F32)
        mu = jnp.mean(o, axis=-1, keepdims=True)
        var = jnp.mean(jnp.square(o - mu), axis=-1, keepdims=True)
        on = (o - mu) * lax.rsqrt(var + EPS)
        g = g_ref[0, rows, :].astype(F32)
        o_ref[0, rows, :] = (on * (g * jax.nn.sigmoid(g))).astype(BF16)
        return carry

    lax.fori_loop(0, nc, out_body, 0)


def _ret_branch(ret, decay_f, decay_b, batch, seq):
    d = RET_HEAD_DIM
    nc = seq // RET_CHUNK
    r3 = ret.reshape(batch, seq, 4 * BRANCH_WIDTH)
    dfb = jnp.broadcast_to(decay_f.astype(F32)[:, None, None], (RET_HEADS, 1, d))
    dbb = jnp.broadcast_to(decay_b.astype(F32)[:, None, None], (RET_HEADS, 1, d))

    def sec(k):
        return pl.BlockSpec((1, seq, d), lambda b, h: (b, 0, k * RET_HEADS + h))

    out = pl.pallas_call(
        functools.partial(_ret_kernel, seq=seq),
        out_shape=jax.ShapeDtypeStruct((batch, seq, BRANCH_WIDTH), BF16),
        grid=(batch, RET_HEADS),
        in_specs=[sec(0), sec(1), sec(2), sec(3),
                  pl.BlockSpec((1, 1, d), lambda b, h: (h, 0, 0)),
                  pl.BlockSpec((1, 1, d), lambda b, h: (h, 0, 0))],
        out_specs=pl.BlockSpec((1, seq, d), lambda b, h: (b, 0, h)),
        scratch_shapes=[pltpu.VMEM((nc, 2 * d, d), F32), pltpu.VMEM((nc, 2 * d, d), BF16)],
        compiler_params=_params(("arbitrary", "arbitrary")),
        name="retention",
    )(r3, r3, r3, r3, dfb, dbb)
    return out.reshape(batch * seq, BRANCH_WIDTH)


def _na_kernel(q_ref, k0_ref, k1_ref, k2_ref, v0_ref, v1_ref, v2_ref, bias_ref, o_ref,
               kw_ref, vw_ref, *, grid_rows):
    rr = NA_ROWS_PER_STEP
    gw = GRID_W
    blk = rr * gw
    kr = NA_ROWS_MAX
    m = pl.program_id(1)
    for idx, (kref, vref) in enumerate(((k0_ref, v0_ref), (k1_ref, v1_ref), (k2_ref, v2_ref))):
        kw_ref[idx * blk:(idx + 1) * blk, :] = kref[0]
        vw_ref[idx * blk:(idx + 1) * blk, :] = vref[0]
    lane = lax.broadcasted_iota(jnp.int32, (gw, 2 * NA_HEAD_DIM), 1)
    low = lane < NA_HEAD_DIM

    def row_body(r, carry):
        r_glob = m * rr + r
        rs = jnp.clip(r_glob - kr // 2, 0, grid_rows - kr)
        off = pl.multiple_of((rs - (m - 1) * rr) * gw, gw)
        didx = r_glob - rs
        qrows = pl.ds(pl.multiple_of(r * gw, gw), gw)
        for jp in range(NA_HEADS // 2):
            cols = slice(jp * 2 * NA_HEAD_DIM, (jp + 1) * 2 * NA_HEAD_DIM)
            q2 = q_ref[0, qrows, cols]
            k2 = kw_ref[pl.ds(off, kr * gw), cols]
            v2 = vw_ref[pl.ds(off, kr * gw), cols]
            outs = []
            for e in range(2):
                qm = jnp.where(low if e == 0 else jnp.logical_not(low), q2, jnp.zeros_like(q2))
                s = lax.dot_general(qm, k2, (((1,), (1,)), ((), ())), preferred_element_type=F32)
                s = s + bias_ref[didx, 2 * jp + e]
                p = jnp.exp(s - jnp.max(s, axis=-1, keepdims=True))
                l = jnp.sum(p, axis=-1, keepdims=True)
                outs.append(jnp.dot(p.astype(BF16), v2, preferred_element_type=F32) / l)
            o_ref[0, qrows, cols] = jnp.where(low, outs[0], outs[1]).astype(BF16)
        return carry

    lax.fori_loop(0, rr, row_body, 0)


def _na_bias(rpb):
    col = jnp.arange(GRID_W)
    cs = jnp.clip(col - NA_COLS // 2, 0, GRID_W - NA_COLS)
    in_win = (col[None, :] >= cs[:, None]) & (col[None, :] < cs[:, None] + NA_COLS)
    dc = jnp.clip(col[None, :] - col[:, None], -(NA_COLS - 1), NA_COLS - 1) + (NA_COLS - 1)
    dl = jnp.arange(NA_ROWS_MAX)
    dr = jnp.arange(NA_ROWS_MAX)[None, :] - dl[:, None] + (NA_ROWS_MAX - 1)
    b = rpb.astype(F32)[:, dr[:, None, :, None], dc[None, :, None, :]]
    b = jnp.where(in_win[None, None, :, None, :], b, NEG_INF)
    return jnp.transpose(b, (1, 0, 2, 3, 4)).reshape(NA_ROWS_MAX, NA_HEADS, GRID_W, NA_ROWS_MAX * GRID_W)


def _na_branch(na, bias, batch, seq):
    w = BRANCH_WIDTH
    grid_rows = seq // GRID_W
    rr = NA_ROWS_PER_STEP
    blk = rr * GRID_W
    nblk = grid_rows // rr
    n3 = na.reshape(batch, seq, 3 * w)

    def kv_spec(col, shift):
        return pl.BlockSpec((1, blk, w), lambda b, m: (b, jnp.clip(m + shift, 0, nblk - 1), col))

    out = pl.pallas_call(
        functools.partial(_na_kernel, grid_rows=grid_rows),
        out_shape=jax.ShapeDtypeStruct((batch, seq, w), BF16),
        grid=(batch, nblk),
        in_specs=[pl.BlockSpec((1, blk, w), lambda b, m: (b, m, 0)),
                  kv_spec(1, -1), kv_spec(1, 0), kv_spec(1, 1),
                  kv_spec(2, -1), kv_spec(2, 0), kv_spec(2, 1),
                  pl.BlockSpec(bias.shape, lambda b, m: (0, 0, 0, 0))],
        out_specs=pl.BlockSpec((1, blk, w), lambda b, m: (b, m, 0)),
        scratch_shapes=[pltpu.VMEM((3 * blk, w), BF16), pltpu.VMEM((3 * blk, w), BF16)],
        compiler_params=_params(("arbitrary", "arbitrary")),
        name="nbr_attn",
    )(n3, n3, n3, n3, n3, n3, n3, bias)
    return out.reshape(batch * seq, w)


def _merge_kernel(x_ref, g_ref, sc_ref, sh_ref, gate_ref, xs_ref,
                  br0, br1, br2, br3, wg0, wg1, wg2, wg3, bg0, bg1, bg2, bg3,
                  wb0, wb1, wb2, wb3, wo_ref, o_ref, h_ref, m_ref, *, nj):
    j = pl.program_id(1)

    @pl.when(j == 0)
    def _():
        h_ref[...] = _norm_mod(x_ref[...], g_ref[...], sc_ref[0], sh_ref[0]).astype(BF16)

    @pl.when(j < nj)
    def _():
        acc = None
        for br, wg, bg, wb in ((br0, wg0, bg0, wb0), (br1, wg1, bg1, wb1),
                               (br2, wg2, bg2, wb2), (br3, wg3, bg3, wb3)):
            gate = jax.nn.sigmoid(jnp.dot(h_ref[...], wg[...], preferred_element_type=F32) + bg[...])
            term = gate * jnp.dot(br[...], wb[0], preferred_element_type=F32)
            acc = term if acc is None else acc + term
        m_ref[j] = acc.astype(BF16)

    @pl.when(j >= nj)
    def _():
        tn = wo_ref.shape[1]
        out = jnp.dot(m_ref[0], wo_ref[0:tn, :], preferred_element_type=F32)
        for kk in range(1, nj):
            out = out + jnp.dot(m_ref[kk], wo_ref[kk * tn:(kk + 1) * tn, :], preferred_element_type=F32)
        o_ref[...] = xs_ref[...] + gate_ref[0] * out


def _merge(x2, g, mods, branches, w_gate, b_gate, w_branch, w_out, seq):
    rows, d = x2.shape
    tn = TN_MERGE
    nj = d // tn
    tps = seq // TM

    def first(j):
        return jnp.minimum(j, nj - 1)

    def second(j):
        return jnp.maximum(j - nj, 0)

    def gate_w_spec(b):
        return pl.BlockSpec((d, tn), lambda i, j: (0, b * nj + first(j)))

    def gate_b_spec(b):
        return pl.BlockSpec((1, tn), lambda i, j: (0, b * nj + first(j)))

    def branch_w_spec(b):
        return pl.BlockSpec((1, BRANCH_WIDTH, tn), lambda i, j: (b, 0, first(j)))

    br_spec = pl.BlockSpec((TM, BRANCH_WIDTH), lambda i, j: (i, 0))
    in_specs = [
        pl.BlockSpec((TM, d), lambda i, j: (i, 0)),
        pl.BlockSpec((1, d), lambda i, j: (0, 0)),
        pl.BlockSpec((1, 1, d), lambda i, j: ((i // tps) * 6 + 1, 0, 0)),
        pl.BlockSpec((1, 1, d), lambda i, j: ((i // tps) * 6 + 0, 0, 0)),
        pl.BlockSpec((1, 1, tn), lambda i, j: ((i // tps) * 6 + 2, 0, second(j))),
        pl.BlockSpec((TM, tn), lambda i, j: (i, second(j))),
        br_spec, br_spec, br_spec, br_spec,
        gate_w_spec(0), gate_w_spec(1), gate_w_spec(2), gate_w_spec(3),
        gate_b_spec(0), gate_b_spec(1), gate_b_spec(2), gate_b_spec(3),
        branch_w_spec(0), branch_w_spec(1), branch_w_spec(2), branch_w_spec(3),
        pl.BlockSpec((d, tn), lambda i, j: (0, second(j))),
    ]
    return pl.pallas_call(
        functools.partial(_merge_kernel, nj=nj),
        out_shape=jax.ShapeDtypeStruct((rows, d), F32),
        grid=(rows // TM, 2 * nj),
        in_specs=in_specs,
        out_specs=pl.BlockSpec((TM, tn), lambda i, j: (i, second(j))),
        scratch_shapes=[pltpu.VMEM((TM, d), BF16), pltpu.VMEM((nj, TM, tn), BF16)],
        compiler_params=_params(("arbitrary", "arbitrary")),
        name="merge",
    )(x2, g, mods, mods, mods, x2, *branches,
      w_gate, w_gate, w_gate, w_gate, b_gate, b_gate, b_gate, b_gate,
      w_branch, w_branch, w_branch, w_branch, w_out)


def _ffn_kernel(x_ref, xp_ref, xn_ref, g_ref, sc_ref, sh_ref, gate_ref,
                wa_ref, wb_ref, cwa_ref, cwb_ref, cba_ref, cbb_ref, wd_ref, gf_ref,
                o_ref, h_ref, *, tps, nf, final):
    i = pl.program_id(0)
    jf = pl.program_id(1)
    tm = x_ref.shape[0]
    ti = i % tps

    @pl.when(jf == 0)
    def _():
        g = g_ref[...]
        sc = sc_ref[0]
        sh = sh_ref[0]
        hp = _norm_mod(xp_ref[...], g, sc, sh)
        hn = _norm_mod(xn_ref[...], g, sc, sh)
        h_ref[0:HALO, :] = jnp.where(ti == 0, 0.0, hp).astype(BF16)
        h_ref[HALO:HALO + tm, :] = _norm_mod(x_ref[...], g, sc, sh).astype(BF16)
        h_ref[HALO + tm:2 * HALO + tm, :] = jnp.where(ti == tps - 1, 0.0, hn).astype(BF16)

    def conv(u, cw_ref, cb_ref):
        n = u.shape[0]
        prev = pltpu.roll(u, 1, 0)[HALO:HALO + tm]
        nxt = pltpu.roll(u, n - 1, 0)[HALO:HALO + tm]
        return prev * cw_ref[0:1, :] + u[HALO:HALO + tm] * cw_ref[1:2, :] + nxt * cw_ref[2:3, :] + cb_ref[...]

    h = h_ref[...]
    a = conv(jnp.dot(h, wa_ref[...], preferred_element_type=F32), cwa_ref, cba_ref)
    b = conv(jnp.dot(h, wb_ref[...], preferred_element_type=F32), cwb_ref, cbb_ref)
    act = (jax.nn.gelu(a, approximate=True) * b).astype(BF16)
    part = jnp.dot(act, wd_ref[...], preferred_element_type=F32)

    @pl.when(jf == 0)
    def _():
        o_ref[...] = part

    @pl.when(jf > 0)
    def _():
        o_ref[...] += part

    @pl.when(jf == nf - 1)
    def _():
        y = x_ref[...] + gate_ref[0] * o_ref[...]
        if final:
            ms = jnp.mean(y * y, axis=-1, keepdims=True)
            y = y * lax.rsqrt(ms + EPS) * gf_ref[...]
        o_ref[...] = y


def _ffn(x2, g, mods, w_up, conv_w, conv_b, w_down, g_final, seq, final):
    rows, d = x2.shape
    tps = seq // TM
    nf = D_FF // TF
    hb = TM // HALO
    nhalo = rows // HALO
    return pl.pallas_call(
        functools.partial(_ffn_kernel, tps=tps, nf=nf, final=final),
        out_shape=jax.ShapeDtypeStruct((rows, d), F32),
        grid=(rows // TM, nf),
        in_specs=[
            pl.BlockSpec((TM, d), lambda i, j: (i, 0)),
            pl.BlockSpec((HALO, d), lambda i, j: (jnp.maximum(i * hb - 1, 0), 0)),
            pl.BlockSpec((HALO, d), lambda i, j: (jnp.minimum((i + 1) * hb, nhalo - 1), 0)),
            pl.BlockSpec((1, d), lambda i, j: (0, 0)),
            pl.BlockSpec((1, 1, d), lambda i, j: ((i // tps) * 6 + 4, 0, 0)),
            pl.BlockSpec((1, 1, d), lambda i, j: ((i // tps) * 6 + 3, 0, 0)),
            pl.BlockSpec((1, 1, d), lambda i, j: ((i // tps) * 6 + 5, 0, 0)),
            pl.BlockSpec((d, TF), lambda i, j: (0, j)),
            pl.BlockSpec((d, TF), lambda i, j: (0, nf + j)),
            pl.BlockSpec((3, TF), lambda i, j: (0, j)),
            pl.BlockSpec((3, TF), lambda i, j: (0, nf + j)),
            pl.BlockSpec((1, TF), lambda i, j: (0, j)),
            pl.BlockSpec((1, TF), lambda i, j: (0, nf + j)),
            pl.BlockSpec((TF, d), lambda i, j: (j, 0)),
            pl.BlockSpec((1, d), lambda i, j: (0, 0)),
        ],
        out_specs=pl.BlockSpec((TM, d), lambda i, j: (i, 0)),
        scratch_shapes=[pltpu.VMEM((TM + 2 * HALO, d), BF16)],
        compiler_params=_params(("arbitrary", "arbitrary")),
        name="conv_ffn",
    )(x2, x2, x2, g, mods, mods, mods, w_up, w_up, conv_w, conv_w, conv_b, conv_b, w_down, g_final)


def _rope_tables(seq):
    half = RET_HEAD_DIM // 2
    inv = ROPE_BASE ** (-jnp.arange(half, dtype=F32) / half)
    ang = jnp.arange(seq, dtype=F32)[:, None] * inv[None, :]
    cos = jnp.cos(ang)
    sin = jnp.sin(ang)
    return jnp.concatenate([cos, cos], axis=1), jnp.concatenate([-sin, sin], axis=1)


def _trunk(x, mods, wts):
    batch, seq, d = x.shape
    x2 = x.reshape(batch * seq, d)
    cos_t, sin_t = _rope_tables(seq)
    fc = _fourier_consts(seq)
    for l in range(DEPTH):
        m = mods[l]
        pool_in, four_in, ret_in, na_in = _in_proj(x2, wts["g_mix"][l], m, cos_t, sin_t, wts["w_in"][l], seq)
        branches = (
            _pool_branch(pool_in, wts["pool_w"][l], wts["pool_scale"][l], seq),
            _fourier_branch(four_in, fc, batch, seq),
            _ret_branch(ret_in, wts["ret_decay_f"][l], wts["ret_decay_b"][l], batch, seq),
            _na_branch(na_in, wts["na_bias"][l], batch, seq),
        )
        x2 = _merge(x2, wts["g_mix"][l], m, branches, wts["w_gate"][l], wts["b_gate"][l],
                    wts["w_branch"][l], wts["w_out"][l], seq)
        x2 = _ffn(x2, wts["g_ffn"][l], m, wts["w_up"][l], wts["conv_w"][l], wts["conv_b"][l],
                  wts["w_down"][l], wts["g_final"], seq, l == DEPTH - 1)
    return x2.reshape(batch, seq, d)


def kernel(x_prompt, x_sample, c_prompt, c_sample, w_ada, b_ada, g_mix, w_in, pool_w, pool_scale,
           ret_decay_f, ret_decay_b, na_rpb, w_branch, w_gate, b_gate, w_out, g_ffn, w_up, conv_w,
           conv_b, w_down, g_final):
    d = D_MODEL
    nb_p = c_prompt.shape[0]
    nb_s = c_sample.shape[0]
    c8 = jnp.concatenate([c_prompt, c_sample, jnp.zeros((8 - nb_p - nb_s, d), F32)], axis=0)
    mod_all = _ada(c8, w_ada, b_ada)
    mods_p = mod_all[:, :nb_p].reshape(DEPTH, nb_p * 6, 1, d)
    mods_s = mod_all[:, nb_p:nb_p + nb_s].reshape(DEPTH, nb_s * 6, 1, d)
    wts = dict(
        g_mix=g_mix.reshape(DEPTH, 1, d),
        w_in=w_in.astype(BF16),
        pool_w=pool_w.astype(BF16),
        pool_scale=pool_scale.reshape(DEPTH, 1, BRANCH_WIDTH),
        ret_decay_f=ret_decay_f,
        ret_decay_b=ret_decay_b,
        na_bias=jnp.stack([_na_bias(na_rpb[l]) for l in range(DEPTH)]),
        w_branch=w_branch.astype(BF16),
        w_gate=w_gate.astype(BF16),
        b_gate=b_gate.reshape(DEPTH, 1, N_BRANCH * d),
        w_out=w_out.astype(BF16),
        g_ffn=g_ffn.reshape(DEPTH, 1, d),
        w_up=w_up.astype(BF16),
        conv_w=conv_w,
        conv_b=conv_b.reshape(DEPTH, 1, 2 * D_FF),
        w_down=w_down.astype(BF16),
        g_final=g_final.reshape(1, d),
    )
    y_prompt = _trunk(x_prompt, mods_p, wts)
    y_sample = _trunk(x_sample, mods_s, wts)
    return (y_prompt, y_sample)
```

```python
import functools
import math

import numpy as np
import jax
import jax.numpy as jnp
from jax import lax
from jax.experimental import pallas as pl
from jax.experimental.pallas import tpu as pltpu

F32 = jnp.float32
BF16 = jnp.bfloat16

D_MODEL = 2048
DEPTH = 4
GRID_W = 64
BRANCH_WIDTH = 512
N_BRANCH = 4
POOL_WINDOWS = (2, 4, 8, 16)
POOL_GROUP = 128
FOURIER_GROUP = 128
RET_HEADS = 4
RET_HEAD_DIM = 128
ROPE_BASE = 10000.0
NA_HEADS = 8
NA_HEAD_DIM = 64
NA_ROWS_MAX = 8
NA_COLS = 16
D_FF = 5632
EPS = 1e-6
NEG_INF = -1e30
IN_WIDTH = 9 * BRANCH_WIDTH

VMEM_LIMIT_BYTES = 56 * 1024 * 1024
BF16_SUBLANES = 16

TM = 512
TN_IN = 512
TN_MERGE = 256
TM_FFN = 512
TF = 512
TN_DOWN = 256
HALO = BF16_SUBLANES
NORM_CHUNK = 32
NORM_UNROLL = 4
FFN_CHUNK = 64
POOL_TILE = 512
POOL_HALO = 8
F_T2 = 128
F1_TILE = 8
F2_TILE = 8
RET_CHUNK = 256
NA_ROWS_PER_STEP = 8


def _params(sem):
    return pltpu.CompilerParams(dimension_semantics=sem, vmem_limit_bytes=VMEM_LIMIT_BYTES)


def _norm_mod(x, g, scale, shift):
    ms = jnp.mean(x * x, axis=-1, keepdims=True)
    return (x * lax.rsqrt(ms + EPS) * g) * (1.0 + scale) + shift


def _norm_rows(x_ref, dst_ref, dst_off, g, scale, shift, keep=None):
    rows = x_ref.shape[0]
    ch = min(rows, NORM_CHUNK)

    def body(c, carry):
        r0 = pl.multiple_of(c * ch, ch)
        h = _norm_mod(x_ref[pl.ds(r0, ch), :], g, scale, shift)
        if keep is not None:
            h = jnp.where(keep, h, 0.0)
        dst_ref[pl.ds(dst_off + r0, ch), :] = h.astype(BF16)
        return carry

    lax.fori_loop(0, rows // ch, body, 0, unroll=min(NORM_UNROLL, rows // ch))


def _col_tiles(w, tn):
    depth, k, n = w.shape
    return w.astype(BF16).reshape(depth, k, n // tn, tn).transpose(0, 2, 1, 3)


def _ada_kernel(c_ref, w_ref, b_ref, o_ref):
    c = c_ref[...]
    s = (c * jax.nn.sigmoid(c)).astype(BF16)
    o_ref[0] = jnp.dot(s, w_ref[0].astype(BF16), preferred_element_type=F32) + b_ref[0]


def _ada(c8, w_ada, b_ada):
    depth, d, n = w_ada.shape
    tn = 1024
    return pl.pallas_call(
        _ada_kernel,
        out_shape=jax.ShapeDtypeStruct((depth, 8, n), F32),
        grid=(depth, n // tn),
        in_specs=[
            pl.BlockSpec((8, d), lambda l, j: (0, 0)),
            pl.BlockSpec((1, d, tn), lambda l, j: (l, 0, j)),
            pl.BlockSpec((1, 1, tn), lambda l, j: (l, 0, j)),
        ],
        out_specs=pl.BlockSpec((1, 8, tn), lambda l, j: (l, 0, j)),
        compiler_params=_params(("arbitrary", "arbitrary")),
        name="ada",
    )(c8, w_ada, b_ada.reshape(depth, 1, n))


def _rope_tile(acc, cos, sin, scale):
    parts = []
    for hh in range(acc.shape[1] // RET_HEAD_DIM):
        ch = acc[:, hh * RET_HEAD_DIM:(hh + 1) * RET_HEAD_DIM]
        parts.append((ch * cos + pltpu.roll(ch, RET_HEAD_DIM // 2, 1) * sin) * scale)
    return jnp.concatenate(parts, axis=1)


def _in_kernel(x_ref, g_ref, sc_ref, sh_ref, cos_ref, sin_ref, w_ref,
               pool_ref, four_ref, ret_ref, na_ref, h_ref):
    j = pl.program_id(1)

    @pl.when(j == 0)
    def _():
        _norm_rows(x_ref, h_ref, 0, g_ref[0], sc_ref[0], sh_ref[0])

    def proj():
        return jnp.dot(h_ref[...], w_ref[0, 0], preferred_element_type=F32)

    @pl.when(j == 0)
    def _():
        pool_ref[...] = proj()

    @pl.when(j == 1)
    def _():
        four_ref[...] = proj().astype(BF16)

    @pl.when(j == 2)
    def _():
        ret_ref[...] = _rope_tile(proj(), cos_ref[...], sin_ref[...], 1.0).astype(BF16)

    @pl.when(j == 3)
    def _():
        ret_ref[...] = _rope_tile(proj(), cos_ref[...], sin_ref[...], RET_HEAD_DIM ** -0.5).astype(BF16)

    @pl.when((j == 4) | (j == 5))
    def _():
        ret_ref[...] = proj().astype(BF16)

    @pl.when(j == 6)
    def _():
        na_ref[...] = (proj() * (NA_HEAD_DIM ** -0.5)).astype(BF16)

    @pl.when(j >= 7)
    def _():
        na_ref[...] = proj().astype(BF16)


def _in_proj(x2, layer, g, mods, cos_t, sin_t, w_in_t, seq):
    rows, d = x2.shape
    tps = seq // TM
    nj = IN_WIDTH // TN_IN

    def mod_spec(k):
        return pl.BlockSpec((1, 1, d), lambda i, j: ((i // tps) * 6 + k, 0, 0))

    return pl.pallas_call(
        _in_kernel,
        out_shape=(
            jax.ShapeDtypeStruct((rows, BRANCH_WIDTH), F32),
            jax.ShapeDtypeStruct((rows, BRANCH_WIDTH), BF16),
            jax.ShapeDtypeStruct((rows, 4 * BRANCH_WIDTH), BF16),
            jax.ShapeDtypeStruct((rows, 3 * BRANCH_WIDTH), BF16),
        ),
        grid=(rows // TM, nj),
        in_specs=[
            pl.BlockSpec((TM, d), lambda i, j: (i, 0)),
            pl.BlockSpec((1, 1, d), lambda i, j: (layer, 0, 0)),
            mod_spec(1),
            mod_spec(0),
            pl.BlockSpec((TM, RET_HEAD_DIM), lambda i, j: (i % tps, 0)),
            pl.BlockSpec((TM, RET_HEAD_DIM), lambda i, j: (i % tps, 0)),
            pl.BlockSpec((1, 1, d, TN_IN), lambda i, j: (layer, j, 0, 0)),
        ],
        out_specs=(
            pl.BlockSpec((TM, TN_IN), lambda i, j: (i, 0)),
            pl.BlockSpec((TM, TN_IN), lambda i, j: (i, 0)),
            pl.BlockSpec((TM, TN_IN), lambda i, j: (i, jnp.clip(j - 2, 0, 3))),
            pl.BlockSpec((TM, TN_IN), lambda i, j: (i, jnp.clip(j - 6, 0, 2))),
        ),
        scratch_shapes=[pltpu.VMEM((TM, d), BF16)],
        compiler_params=_params(("arbitrary", "arbitrary")),
        name="in_proj",
    )(x2, g, mods, mods, cos_t, sin_t, w_in_t)


def _pool_kernel(a_ref, ap_ref, an_ref, w_ref, s_ref, o_ref, ext_ref, *, tps, seq):
    tp = a_ref.shape[0]
    ti = pl.program_id(0) % tps
    ext_ref[0:POOL_HALO, :] = jnp.where(ti == 0, 0.0, ap_ref[...])
    ext_ref[POOL_HALO:POOL_HALO + tp, :] = a_ref[...]
    ext_ref[POOL_HALO + tp:2 * POOL_HALO + tp, :] = jnp.where(ti == tps - 1, 0.0, an_ref[...])
    t = ti * tp + lax.broadcasted_iota(jnp.int32, (tp, 1), 0)
    for gi, w in enumerate(POOL_WINDOWS):
        cols = slice(gi * POOL_GROUP, (gi + 1) * POOL_GROUP)
        half = w // 2
        acc = ext_ref[pl.ds(POOL_HALO - half, tp), cols]
        for k in range(-half + 1, half):
            acc = acc + ext_ref[pl.ds(POOL_HALO + k, tp), cols]
        count = (jnp.minimum(t + half, seq) - jnp.maximum(t - half, 0)).astype(F32)
        pooled = acc / count - a_ref[:, cols]
        y = jnp.dot(pooled.astype(BF16), w_ref[0, gi], preferred_element_type=F32) * s_ref[0, :, cols]
        o_ref[:, cols] = y.astype(BF16)


def _pool_branch(a, layer, pool_w, pool_scale, seq):
    rows, w = a.shape
    tp = POOL_TILE
    tps = seq // tp
    hb = tp // POOL_HALO
    nhalo = rows // POOL_HALO
    return pl.pallas_call(
        functools.partial(_pool_kernel, tps=tps, seq=seq),
        out_shape=jax.ShapeDtypeStruct((rows, w), BF16),
        grid=(rows // tp,),
        in_specs=[
            pl.BlockSpec((tp, w), lambda i: (i, 0)),
            pl.BlockSpec((POOL_HALO, w), lambda i: (jnp.maximum(i * hb - 1, 0), 0)),
            pl.BlockSpec((POOL_HALO, w), lambda i: (jnp.minimum((i + 1) * hb, nhalo - 1), 0)),
            pl.BlockSpec((1, len(POOL_WINDOWS), POOL_GROUP, POOL_GROUP), lambda i: (layer, 0, 0, 0)),
            pl.BlockSpec((1, 1, w), lambda i: (layer, 0, 0)),
        ],
        out_specs=pl.BlockSpec((tp, w), lambda i: (i, 0)),
        scratch_shapes=[pltpu.VMEM((tp + 2 * POOL_HALO, w), F32)],
        compiler_params=_params(("arbitrary",)),
        name="pool",
    )(a, a, a, pool_w, pool_scale)


def _f1_kernel(x_ref, m1_ref, tc_ref, ts_ref, yr_ref, yi_ref):
    n1 = x_ref.shape[1]
    y = jnp.dot(m1_ref[...], x_ref[0], preferred_element_type=F32)
    yr = y[:n1]
    yi = y[n1:]
    c = tc_ref[...]
    s = ts_ref[...]
    yr_ref[0] = (yr * c + yi * s).astype(BF16)
    yi_ref[0] = (yi * c - yr * s).astype(BF16)


def _f2_kernel(yr_ref, yi_ref, ma_ref, mb_ref, cc_ref, cs_ref, o_ref, zr_ref, zi_ref, *, scale):
    n2 = F_T2
    w = o_ref.shape[2] // F2_TILE
    for s in range(F2_TILE):
        rows = slice(s * n2, (s + 1) * n2)
        z = (jnp.dot(ma_ref[...], yr_ref[0, rows, :], preferred_element_type=F32)
             + jnp.dot(mb_ref[...], yi_ref[0, rows, :], preferred_element_type=F32))
        zr_ref[rows, :] = z[:n2].astype(BF16)
        zi_ref[rows, :] = z[n2:].astype(BF16)
    o = (jnp.dot(zr_ref[...], cc_ref[...], preferred_element_type=F32)
         + jnp.dot(zi_ref[...], cs_ref[...], preferred_element_type=F32)) * scale
    for s in range(F2_TILE):
        o_ref[0, :, s * w:(s + 1) * w] = o[s * n2:(s + 1) * n2].astype(BF16)


def _dft_cos_sin(n):
    k = np.arange(n)
    ang = 2.0 * np.pi * ((k[:, None] * k[None, :]) % n) / n
    return np.cos(ang), np.sin(ang)


def _fourier_consts(seq):
    n2 = F_T2
    n1 = seq // n2
    c1, s1 = _dft_cos_sin(n1)
    c2, s2 = _dft_cos_sin(n2)
    cg, sg = _dft_cos_sin(FOURIER_GROUP)
    eye = np.eye(BRANCH_WIDTH // FOURIER_GROUP)
    m1 = jnp.asarray(np.concatenate([c1, -s1], axis=0), BF16)
    ma = jnp.asarray(np.concatenate([c2, -s2], axis=0), BF16)
    mb = jnp.asarray(np.concatenate([s2, c2], axis=0), BF16)
    cc = jnp.asarray(np.kron(eye, cg), BF16)
    cs = jnp.asarray(np.kron(eye, sg), BF16)
    prod = (jnp.arange(n1, dtype=jnp.int32)[:, None] * jnp.arange(n2, dtype=jnp.int32)[None, :]) % seq
    ang = prod.astype(F32) * (2.0 * math.pi / seq)
    shape = (n1, n2, BRANCH_WIDTH)
    tc = jnp.broadcast_to(jnp.cos(ang)[:, :, None], shape).reshape(n1, n2 * BRANCH_WIDTH)
    ts = jnp.broadcast_to(jnp.sin(ang)[:, :, None], shape).reshape(n1, n2 * BRANCH_WIDTH)
    return dict(m1=m1, ma=ma, mb=mb, cc=cc, cs=cs, tc=tc, ts=ts)


def _fourier_branch(f, fc, batch, seq):
    w = BRANCH_WIDTH
    n2 = F_T2
    n1 = seq // n2
    blk = F1_TILE * w
    x3 = f.reshape(batch, n1, n2 * w)
    yr, yi = pl.pallas_call(
        _f1_kernel,
        out_shape=(jax.ShapeDtypeStruct((batch, n1, n2 * w), BF16),) * 2,
        grid=(n2 // F1_TILE, batch),
        in_specs=[
            pl.BlockSpec((1, n1, blk), lambda j, b: (b, 0, j)),
            pl.BlockSpec((2 * n1, n1), lambda j, b: (0, 0)),
            pl.BlockSpec((n1, blk), lambda j, b: (0, j)),
            pl.BlockSpec((n1, blk), lambda j, b: (0, j)),
        ],
        out_specs=(pl.BlockSpec((1, n1, blk), lambda j, b: (b, 0, j)),) * 2,
        compiler_params=_params(("arbitrary", "arbitrary")),
        name="fourier1",
    )(x3, fc["m1"], fc["tc"], fc["ts"])
    yr = yr.reshape(batch, n1 * n2, w)
    yi = yi.reshape(batch, n1 * n2, w)
    rows = F2_TILE * n2
    out = pl.pallas_call(
        functools.partial(_f2_kernel, scale=1.0 / math.sqrt(seq * FOURIER_GROUP)),
        out_shape=jax.ShapeDtypeStruct((batch, n2, n1 * w), BF16),
        grid=(batch, n1 // F2_TILE),
        in_specs=[
            pl.BlockSpec((1, rows, w), lambda b, j: (b, j, 0)),
            pl.BlockSpec((1, rows, w), lambda b, j: (b, j, 0)),
            pl.BlockSpec((2 * n2, n2), lambda b, j: (0, 0)),
            pl.BlockSpec((2 * n2, n2), lambda b, j: (0, 0)),
            pl.BlockSpec((w, w), lambda b, j: (0, 0)),
            pl.BlockSpec((w, w), lambda b, j: (0, 0)),
        ],
        out_specs=pl.BlockSpec((1, n2, F2_TILE * w), lambda b, j: (b, 0, j)),
        scratch_shapes=[pltpu.VMEM((rows, w), BF16), pltpu.VMEM((rows, w), BF16)],
        compiler_params=_params(("arbitrary", "arbitrary")),
        name="fourier2",
    )(yr, yi, fc["ma"], fc["mb"], fc["cc"], fc["cs"])
    return out.reshape(batch * seq, w)


def _log_sigmoid(x):
    return jnp.minimum(x, 0.0) - jnp.log1p(jnp.exp(-jnp.abs(x)))


def _ret_kernel(q_ref, k_ref, v_ref, g_ref, df_ref, db_ref, o_ref, kv_ref, r_ref, *, seq):
    c = RET_CHUNK
    d = RET_HEAD_DIM
    nc = seq // c
    lgf = _log_sigmoid(df_ref[0, 0])
    lgb = _log_sigmoid(db_ref[0, 0])
    ii = lax.broadcasted_iota(jnp.int32, (c, 1), 0).astype(F32)
    kdf = jnp.exp(lgf * (c - 1.0 - ii))
    kdb = jnp.exp(lgb * ii)
    qdf = jnp.exp(lgf * (ii + 1.0))
    qdb = jnp.exp(lgb * (c - ii))
    diff = (lax.broadcasted_iota(jnp.int32, (c, c), 0)
            - lax.broadcasted_iota(jnp.int32, (c, c), 1)).astype(F32)
    lgf_c = jnp.concatenate([lgf] * (c // d), axis=1)
    lgb_c = jnp.concatenate([lgb] * (c // d), axis=1)
    dmat = jnp.where(diff >= 0.0, jnp.exp(lgf_c * jnp.maximum(diff, 0.0)),
                     jnp.exp(lgb_c * jnp.maximum(-diff, 0.0)))
    chunk_f = jnp.exp(lgf * float(c))
    chunk_b = jnp.exp(lgb * float(c))

    def kv_body(n, carry):
        rows = pl.ds(pl.multiple_of(n * c, c), c)
        kc = k_ref[0, rows, :].astype(F32)
        kd = jnp.concatenate([kc * kdf, kc * kdb], axis=1).astype(BF16)
        kv_ref[n] = lax.dot_general(kd, v_ref[0, rows, :], (((0,), (0,)), ((), ())),
                                    preferred_element_type=F32)
        return carry

    lax.fori_loop(0, nc, kv_body, 0, unroll=2)

    def fwd_body(n, state):
        r_ref[n, 0:d, :] = state.astype(BF16)
        return chunk_f * state + kv_ref[n, 0:d, :]

    lax.fori_loop(0, nc, fwd_body, jnp.zeros((d, d), F32))

    def bwd_body(m, state):
        n = nc - 1 - m
        r_ref[n, d:2 * d, :] = state.astype(BF16)
        return chunk_b * state + kv_ref[n, d:2 * d, :]

    lax.fori_loop(0, nc, bwd_body, jnp.zeros((d, d), F32))

    def out_body(t, carry):
        staged = []
        for n in (2 * t, 2 * t + 1):
            rows = pl.ds(pl.multiple_of(n * c, c), c)
            qb = q_ref[0, rows, :]
            s = lax.dot_general(qb, k_ref[0, rows, :], (((1,), (1,)), ((), ())),
                                preferred_element_type=F32)
            qc = qb.astype(F32)
            qd = jnp.concatenate([qc * qdf, qc * qdb], axis=1).astype(BF16)
            staged.append((rows, s, jnp.dot(qd, r_ref[n], preferred_element_type=F32)))
        for rows, s, cross in staged:
            o = jnp.dot((s * dmat).astype(BF16), v_ref[0, rows, :], preferred_element_type=F32) + cross
            mu = jnp.mean(o, axis=-1, keepdims=True)
            var = jnp.mean(jnp.square(o - mu), axis=-1, keepdims=True)
            on = (o - mu) * lax.rsqrt(var + EPS)
            g = g_ref[0, rows, :].astype(F32)
            o_ref[0, rows, :] = (on * (g * jax.nn.sigmoid(g))).astype(BF16)
        return carry

    lax.fori_loop(0, nc // 2, out_body, 0)


def _ret_branch(ret, layer, decay_f, decay_b, batch, seq):
    d = RET_HEAD_DIM
    nc = seq // RET_CHUNK
    r3 = ret.reshape(batch, seq, 4 * BRANCH_WIDTH)

    def sec(k):
        return pl.BlockSpec((1, seq, d), lambda b, h: (b, 0, k * RET_HEADS + h))

    decay_spec = pl.BlockSpec((1, 1, 1, d), lambda b, h: (layer, h, 0, 0))
    out = pl.pallas_call(
        functools.partial(_ret_kernel, seq=seq),
        out_shape=jax.ShapeDtypeStruct((batch, seq, BRANCH_WIDTH), BF16),
        grid=(batch, RET_HEADS),
        in_specs=[sec(0), sec(1), sec(2), sec(3), decay_spec, decay_spec],
        out_specs=pl.BlockSpec((1, seq, d), lambda b, h: (b, 0, h)),
        scratch_shapes=[pltpu.VMEM((nc, 2 * d, d), F32), pltpu.VMEM((nc, 2 * d, d), BF16)],
        compiler_params=_params(("arbitrary", "arbitrary")),
        name="retention",
    )(r3, r3, r3, r3, decay_f, decay_b)
    return out.reshape(batch * seq, BRANCH_WIDTH)


def _na_kernel(q_ref, k0_ref, k1_ref, k2_ref, v0_ref, v1_ref, v2_ref, bias_ref, o_ref,
               kw_ref, vw_ref, *, grid_rows):
    rr = NA_ROWS_PER_STEP
    gw = GRID_W
    blk = rr * gw
    kr = NA_ROWS_MAX
    m = pl.program_id(1)
    for idx, (kref, vref) in enumerate(((k0_ref, v0_ref), (k1_ref, v1_ref), (k2_ref, v2_ref))):
        kw_ref[idx * blk:(idx + 1) * blk, :] = kref[0]
        vw_ref[idx * blk:(idx + 1) * blk, :] = vref[0]
    lane = lax.broadcasted_iota(jnp.int32, (gw, 2 * NA_HEAD_DIM), 1)
    low = lane < NA_HEAD_DIM

    pair_cols = [slice(jp * 2 * NA_HEAD_DIM, (jp + 1) * 2 * NA_HEAD_DIM) for jp in range(NA_HEADS // 2)]

    def window(r):
        r_glob = m * rr + r
        rs = jnp.clip(r_glob - kr // 2, 0, grid_rows - kr)
        krows = pl.ds(pl.multiple_of((rs - (m - 1) * rr) * gw, gw), kr * gw)
        qrows = pl.ds(pl.multiple_of(r * gw, gw), gw)
        return qrows, krows, r_glob - rs

    def scores(r):
        qrows, krows, didx = window(r)
        out = []
        for jp, cols in enumerate(pair_cols):
            q2 = q_ref[0, qrows, cols]
            k2 = kw_ref[krows, cols]
            for e in range(2):
                qm = jnp.where(low if e == 0 else jnp.logical_not(low), q2, jnp.zeros_like(q2))
                s = lax.dot_general(qm, k2, (((1,), (1,)), ((), ())), preferred_element_type=F32)
                out.append(s + bias_ref[0, didx, 2 * jp + e])
        return out

    def softmax(ss):
        out = []
        for s in ss:
            p = jnp.exp(s - jnp.max(s, axis=-1, keepdims=True))
            out.append((p.astype(BF16), jnp.sum(p, axis=-1, keepdims=True)))
        return out

    def values(r, ps):
        qrows, krows, _ = window(r)
        for jp, cols in enumerate(pair_cols):
            v2 = vw_ref[krows, cols]
            outs = [jnp.dot(p, v2, preferred_element_type=F32) / l for p, l in ps[2 * jp:2 * jp + 2]]
            o_ref[0, qrows, cols] = jnp.where(low, outs[0], outs[1]).astype(BF16)

    def pair_body(t, carry):
        s0 = scores(2 * t)
        s1 = scores(2 * t + 1)
        values(2 * t, softmax(s0))
        values(2 * t + 1, softmax(s1))
        return carry

    lax.fori_loop(0, rr // 2, pair_body, 0)


def _na_bias(rpb):
    rpb = rpb.astype(F32)
    per_q = []
    for qc in range(GRID_W):
        cs = min(max(qc - NA_COLS // 2, 0), GRID_W - NA_COLS)
        lo = (NA_COLS - 1) - qc + cs
        seg = rpb[..., lo:lo + NA_COLS]
        per_q.append(jnp.pad(seg, ((0, 0),) * 3 + ((cs, GRID_W - NA_COLS - cs),), constant_values=NEG_INF))
    toep = jnp.stack(per_q, axis=3)
    per_dl = []
    for dl in range(NA_ROWS_MAX):
        lo = NA_ROWS_MAX - 1 - dl
        win = toep[:, :, lo:lo + NA_ROWS_MAX]
        per_dl.append(jnp.transpose(win, (0, 1, 3, 2, 4)).reshape(
            rpb.shape[0], NA_HEADS, GRID_W, NA_ROWS_MAX * GRID_W))
    return jnp.stack(per_dl, axis=1)


def _na_branch(na, layer, bias, batch, seq):
    w = BRANCH_WIDTH
    grid_rows = seq // GRID_W
    rr = NA_ROWS_PER_STEP
    blk = rr * GRID_W
    nblk = grid_rows // rr
    n3 = na.reshape(batch, seq, 3 * w)

    def kv_spec(col, shift):
        return pl.BlockSpec((1, blk, w), lambda b, m: (b, jnp.clip(m + shift, 0, nblk - 1), col))

    out = pl.pallas_call(
        functools.partial(_na_kernel, grid_rows=grid_rows),
        out_shape=jax.ShapeDtypeStruct((batch, seq, w), BF16),
        grid=(batch, nblk),
        in_specs=[pl.BlockSpec((1, blk, w), lambda b, m: (b, m, 0)),
                  kv_spec(1, -1), kv_spec(1, 0), kv_spec(1, 1),
                  kv_spec(2, -1), kv_spec(2, 0), kv_spec(2, 1),
                  pl.BlockSpec((1,) + bias.shape[1:], lambda b, m: (layer, 0, 0, 0, 0))],
        out_specs=pl.BlockSpec((1, blk, w), lambda b, m: (b, m, 0)),
        scratch_shapes=[pltpu.VMEM((3 * blk, w), BF16), pltpu.VMEM((3 * blk, w), BF16)],
        compiler_params=_params(("arbitrary", "arbitrary")),
        name="nbr_attn",
    )(n3, n3, n3, n3, n3, n3, n3, bias)
    return out.reshape(batch * seq, w)


def _merge_kernel(x_ref, g_ref, sc_ref, sh_ref, gate_ref, xs_ref,
                  br0, br1, br2, br3, wg0, wg1, wg2, wg3, bg0, bg1, bg2, bg3,
                  wb0, wb1, wb2, wb3, wo_ref, o_ref, h_ref, m_ref, *, nj):
    j = pl.program_id(1)

    @pl.when(j == 0)
    def _():
        _norm_rows(x_ref, h_ref, 0, g_ref[0], sc_ref[0], sh_ref[0])

    @pl.when(j < nj)
    def _():
        acc = None
        for br, wg, bg, wb in ((br0, wg0, bg0, wb0), (br1, wg1, bg1, wb1),
                               (br2, wg2, bg2, wb2), (br3, wg3, bg3, wb3)):
            gate = jax.nn.sigmoid(jnp.dot(h_ref[...], wg[0, 0], preferred_element_type=F32) + bg[0])
            term = gate * jnp.dot(br[...], wb[0, 0, 0], preferred_element_type=F32)
            acc = term if acc is None else acc + term
        m_ref[j] = acc.astype(BF16)

    @pl.when(j >= nj)
    def _():
        tn = wo_ref.shape[3]
        out = jnp.dot(m_ref[0], wo_ref[0, 0, 0:tn, :], preferred_element_type=F32)
        for kk in range(1, nj):
            out = out + jnp.dot(m_ref[kk], wo_ref[0, 0, kk * tn:(kk + 1) * tn, :], preferred_element_type=F32)
        o_ref[...] = xs_ref[...] + gate_ref[0] * out


def _merge(x2, layer, g, mods, branches, w_gate_t, b_gate, w_branch_t, w_out_t, seq):
    rows, d = x2.shape
    tn = TN_MERGE
    nj = d // tn
    tps = seq // TM

    def first(j):
        return jnp.minimum(j, nj - 1)

    def second(j):
        return jnp.maximum(j - nj, 0)

    def gate_w_spec(b):
        return pl.BlockSpec((1, 1, d, tn), lambda i, j: (layer, b * nj + first(j), 0, 0))

    def gate_b_spec(b):
        return pl.BlockSpec((1, 1, tn), lambda i, j: (layer, 0, b * nj + first(j)))

    def branch_w_spec(b):
        return pl.BlockSpec((1, 1, 1, BRANCH_WIDTH, tn), lambda i, j: (layer, b, first(j), 0, 0))

    br_spec = pl.BlockSpec((TM, BRANCH_WIDTH), lambda i, j: (i, 0))
    in_specs = [
        pl.BlockSpec((TM, d), lambda i, j: (i, 0)),
        pl.BlockSpec((1, 1, d), lambda i, j: (layer, 0, 0)),
        pl.BlockSpec((1, 1, d), lambda i, j: ((i // tps) * 6 + 1, 0, 0)),
        pl.BlockSpec((1, 1, d), lambda i, j: ((i // tps) * 6 + 0, 0, 0)),
        pl.BlockSpec((1, 1, tn), lambda i, j: ((i // tps) * 6 + 2, 0, second(j))),
        pl.BlockSpec((TM, tn), lambda i, j: (i, second(j))),
        br_spec, br_spec, br_spec, br_spec,
        gate_w_spec(0), gate_w_spec(1), gate_w_spec(2), gate_w_spec(3),
        gate_b_spec(0), gate_b_spec(1), gate_b_spec(2), gate_b_spec(3),
        branch_w_spec(0), branch_w_spec(1), branch_w_spec(2), branch_w_spec(3),
        pl.BlockSpec((1, 1, d, tn), lambda i, j: (layer, second(j), 0, 0)),
    ]
    return pl.pallas_call(
        functools.partial(_merge_kernel, nj=nj),
        out_shape=jax.ShapeDtypeStruct((rows, d), F32),
        grid=(rows // TM, 2 * nj),
        in_specs=in_specs,
        out_specs=pl.BlockSpec((TM, tn), lambda i, j: (i, second(j))),
        scratch_shapes=[pltpu.VMEM((TM, d), BF16), pltpu.VMEM((nj, TM, tn), BF16)],
        compiler_params=_params(("arbitrary", "arbitrary")),
        name="merge",
    )(x2, g, mods, mods, mods, x2, *branches,
      w_gate_t, w_gate_t, w_gate_t, w_gate_t, b_gate, b_gate, b_gate, b_gate,
      w_branch_t, w_branch_t, w_branch_t, w_branch_t, w_out_t)


def _ffn_kernel(x_ref, xp_ref, xn_ref, g_ref, sc_ref, sh_ref, gate_ref, xs_ref,
                wa_ref, wb_ref, cwa_ref, cwb_ref, cba_ref, cbb_ref, wd_ref,
                o_ref, h_ref, ua0_ref, ub0_ref, ua1_ref, ub1_ref, act_ref, *, tps, nf):
    i = pl.program_id(0)
    j = pl.program_id(1)
    tm = x_ref.shape[0]
    tf = ua0_ref.shape[1]
    ti = i % tps

    ch = FFN_CHUNK
    ext = ch + 2 * 8

    bufs = ((ua0_ref, ub0_ref), (ua1_ref, ub1_ref))

    def up(slot):
        h = h_ref[...]
        bufs[slot][0][...] = jnp.dot(h, wa_ref[0, 0], preferred_element_type=F32)
        bufs[slot][1][...] = jnp.dot(h, wb_ref[0, 0], preferred_element_type=F32)

    def conv(u_ref, cw_ref, cb_ref, r0):
        u = u_ref[r0 + HALO - 8:r0 + HALO - 8 + ext, :]
        prev = pltpu.roll(u, 1, 0)[8:8 + ch]
        nxt = pltpu.roll(u, ext - 1, 0)[8:8 + ch]
        return (prev * cw_ref[0, 0:1, :] + u[8:8 + ch] * cw_ref[0, 1:2, :]
                + nxt * cw_ref[0, 2:3, :] + cb_ref[0])

    def activate(slot, tile):
        for r0 in range(0, tm, ch):
            a = conv(bufs[slot][0], cwa_ref, cba_ref, r0)
            b = conv(bufs[slot][1], cwb_ref, cbb_ref, r0)
            act_ref[tile, r0:r0 + ch, :] = (jax.nn.gelu(a, approximate=True) * b).astype(BF16)

    def down():
        out = jnp.dot(act_ref[0], wd_ref[0, 0, 0:tf, :], preferred_element_type=F32)
        for kk in range(1, nf):
            out = out + jnp.dot(act_ref[kk], wd_ref[0, 0, kk * tf:(kk + 1) * tf, :],
                                preferred_element_type=F32)
        o_ref[...] = xs_ref[...] + gate_ref[0] * out

    @pl.when(j == 0)
    def _():
        g = g_ref[0]
        sc = sc_ref[0]
        sh = sh_ref[0]
        _norm_rows(xp_ref, h_ref, 0, g, sc, sh, keep=ti != 0)
        _norm_rows(x_ref, h_ref, HALO, g, sc, sh)
        _norm_rows(xn_ref, h_ref, HALO + tm, g, sc, sh, keep=ti != tps - 1)
        up(0)

    for parity in range(2):
        @pl.when((j >= 1) & (j < nf) & (j % 2 == parity))
        def _():
            activate(1 - parity, j - 1)
            up(parity)

    @pl.when(j == nf)
    def _():
        activate((nf - 1) % 2, nf - 1)
        down()

    @pl.when(j > nf)
    def _():
        down()


def _ffn(x2, layer, g, mods, w_up_t, conv_w, conv_b, w_down_t, seq):
    rows, d = x2.shape
    tm = TM_FFN
    tps = seq // tm
    nf = D_FF // TF
    tn = TN_DOWN
    nd = d // tn
    hb = tm // HALO
    nhalo = rows // HALO

    def first(j):
        return jnp.minimum(j, nf - 1)

    def second(j):
        return jnp.maximum(j - nf, 0)

    def lagged(j):
        return jnp.clip(j - 1, 0, nf - 1)

    return pl.pallas_call(
        functools.partial(_ffn_kernel, tps=tps, nf=nf),
        out_shape=jax.ShapeDtypeStruct((rows, d), F32),
        grid=(rows // tm, nf + nd),
        in_specs=[
            pl.BlockSpec((tm, d), lambda i, j: (i, 0)),
            pl.BlockSpec((HALO, d), lambda i, j: (jnp.maximum(i * hb - 1, 0), 0)),
            pl.BlockSpec((HALO, d), lambda i, j: (jnp.minimum((i + 1) * hb, nhalo - 1), 0)),
            pl.BlockSpec((1, 1, d), lambda i, j: (layer, 0, 0)),
            pl.BlockSpec((1, 1, d), lambda i, j: ((i // tps) * 6 + 4, 0, 0)),
            pl.BlockSpec((1, 1, d), lambda i, j: ((i // tps) * 6 + 3, 0, 0)),
            pl.BlockSpec((1, 1, tn), lambda i, j: ((i // tps) * 6 + 5, 0, second(j))),
            pl.BlockSpec((tm, tn), lambda i, j: (i, second(j))),
            pl.BlockSpec((1, 1, d, TF), lambda i, j: (layer, first(j), 0, 0)),
            pl.BlockSpec((1, 1, d, TF), lambda i, j: (layer, nf + first(j), 0, 0)),
            pl.BlockSpec((1, 3, TF), lambda i, j: (layer, 0, lagged(j))),
            pl.BlockSpec((1, 3, TF), lambda i, j: (layer, 0, nf + lagged(j))),
            pl.BlockSpec((1, 1, TF), lambda i, j: (layer, 0, lagged(j))),
            pl.BlockSpec((1, 1, TF), lambda i, j: (layer, 0, nf + lagged(j))),
            pl.BlockSpec((1, 1, D_FF, tn), lambda i, j: (layer, second(j), 0, 0)),
        ],
        out_specs=pl.BlockSpec((tm, tn), lambda i, j: (i, second(j))),
        scratch_shapes=[pltpu.VMEM((tm + 2 * HALO, d), BF16),
                        pltpu.VMEM((tm + 2 * HALO, TF), F32),
                        pltpu.VMEM((tm + 2 * HALO, TF), F32),
                        pltpu.VMEM((tm + 2 * HALO, TF), F32),
                        pltpu.VMEM((tm + 2 * HALO, TF), F32),
                        pltpu.VMEM((nf, tm, TF), BF16)],
        compiler_params=_params(("arbitrary", "arbitrary")),
        name="conv_ffn",
    )(x2, x2, x2, g, mods, mods, mods, x2, w_up_t, w_up_t, conv_w, conv_w, conv_b, conv_b, w_down_t)


def _final_kernel(x_ref, g_ref, o_ref):
    x = x_ref[...]
    ms = jnp.mean(x * x, axis=-1, keepdims=True)
    o_ref[...] = x * lax.rsqrt(ms + EPS) * g_ref[...]


def _final_norm(x2, g_final):
    rows, d = x2.shape
    tr = 256
    return pl.pallas_call(
        _final_kernel,
        out_shape=jax.ShapeDtypeStruct((rows, d), F32),
        grid=(rows // tr,),
        in_specs=[pl.BlockSpec((tr, d), lambda i: (i, 0)), pl.BlockSpec((1, d), lambda i: (0, 0))],
        out_specs=pl.BlockSpec((tr, d), lambda i: (i, 0)),
        compiler_params=_params(("arbitrary",)),
        name="final_norm",
    )(x2, g_final)


def _rope_tables(seq):
    half = RET_HEAD_DIM // 2
    inv = ROPE_BASE ** (-jnp.arange(half, dtype=F32) / half)
    ang = jnp.arange(seq, dtype=F32)[:, None] * inv[None, :]
    cos = jnp.cos(ang)
    sin = jnp.sin(ang)
    return jnp.concatenate([cos, cos], axis=1), jnp.concatenate([-sin, sin], axis=1)


def _trunk(x, mods, wts):
    batch, seq, d = x.shape
    x2 = x.reshape(batch * seq, d)
    cos_t, sin_t = _rope_tables(seq)
    fc = _fourier_consts(seq)
    for l in range(DEPTH):
        m = mods[l]
        pool_in, four_in, ret_in, na_in = _in_proj(x2, l, wts["g_mix"], m, cos_t, sin_t, wts["w_in"], seq)
        branches = (
            _pool_branch(pool_in, l, wts["pool_w"], wts["pool_scale"], seq),
            _fourier_branch(four_in, fc, batch, seq),
            _ret_branch(ret_in, l, wts["ret_decay_f"], wts["ret_decay_b"], batch, seq),
            _na_branch(na_in, l, wts["na_bias"], batch, seq),
        )
        x2 = _merge(x2, l, wts["g_mix"], m, branches, wts["w_gate"], wts["b_gate"],
                    wts["w_branch"], wts["w_out"], seq)
        x2 = _ffn(x2, l, wts["g_ffn"], m, wts["w_up"], wts["conv_w"], wts["conv_b"], wts["w_down"], seq)
    return _final_norm(x2, wts["g_final"]).reshape(batch, seq, d)


def kernel(x_prompt, x_sample, c_prompt, c_sample, w_ada, b_ada, g_mix, w_in, pool_w, pool_scale,
           ret_decay_f, ret_decay_b, na_rpb, w_branch, w_gate, b_gate, w_out, g_ffn, w_up, conv_w,
           conv_b, w_down, g_final):
    d = D_MODEL
    nb_p = c_prompt.shape[0]
    nb_s = c_sample.shape[0]
    c8 = jnp.concatenate([c_prompt, c_sample, jnp.zeros((8 - nb_p - nb_s, d), F32)], axis=0)
    mod_all = _ada(c8, w_ada, b_ada)
    mods_p = mod_all[:, :nb_p].reshape(DEPTH, nb_p * 6, 1, d)
    mods_s = mod_all[:, nb_p:nb_p + nb_s].reshape(DEPTH, nb_s * 6, 1, d)
    nj = d // TN_MERGE
    decay_shape = (DEPTH, RET_HEADS, 1, RET_HEAD_DIM)
    wts = dict(
        g_mix=g_mix.reshape(DEPTH, 1, d),
        w_in=_col_tiles(w_in, TN_IN),
        pool_w=pool_w.astype(BF16),
        pool_scale=pool_scale.reshape(DEPTH, 1, BRANCH_WIDTH),
        ret_decay_f=jnp.broadcast_to(ret_decay_f.astype(F32)[:, :, None, None], decay_shape),
        ret_decay_b=jnp.broadcast_to(ret_decay_b.astype(F32)[:, :, None, None], decay_shape),
        na_bias=_na_bias(na_rpb),
        w_branch=w_branch.astype(BF16).reshape(DEPTH, N_BRANCH, BRANCH_WIDTH, nj, TN_MERGE).transpose(0, 1, 3, 2, 4),
        w_gate=_col_tiles(w_gate, TN_MERGE),
        b_gate=b_gate.reshape(DEPTH, 1, N_BRANCH * d),
        w_out=_col_tiles(w_out, TN_MERGE),
        g_ffn=g_ffn.reshape(DEPTH, 1, d),
        w_up=_col_tiles(w_up, TF),
        conv_w=conv_w,
        conv_b=conv_b.reshape(DEPTH, 1, 2 * D_FF),
        w_down=_col_tiles(w_down, TN_DOWN),
        g_final=g_final.reshape(1, d),
    )
    y_prompt = _trunk(x_prompt, mods_p, wts)
    y_sample = _trunk(x_sample, mods_s, wts)
    return (y_prompt, y_sample)
```

```python
import functools
import math

import numpy as np
import jax
import jax.numpy as jnp
from jax import lax
from jax.experimental import pallas as pl
from jax.experimental.pallas import tpu as pltpu

F32 = jnp.float32
BF16 = jnp.bfloat16

D_MODEL = 2048
DEPTH = 4
GRID_W = 64
BRANCH_WIDTH = 512
N_BRANCH = 4
POOL_WINDOWS = (2, 4, 8, 16)
POOL_GROUP = 128
FOURIER_GROUP = 128
RET_HEADS = 4
RET_HEAD_DIM = 128
ROPE_BASE = 10000.0
NA_HEADS = 8
NA_HEAD_DIM = 64
NA_ROWS_MAX = 8
NA_COLS = 16
D_FF = 5632
EPS = 1e-6
NEG_INF = -1e30
IN_WIDTH = 9 * BRANCH_WIDTH

VMEM_LIMIT_BYTES = 56 * 1024 * 1024
BF16_SUBLANES = 16

TM = 1024
TN_IN = 512
TN_MERGE = 256
TM_FFN = 1024
TF = 256
TN_DOWN = 256
HALO = BF16_SUBLANES
NORM_CHUNK = 32
NORM_UNROLL = 4
FFN_CHUNK = 64
POOL_TILE = 512
POOL_HALO = 8
F_T2 = 128
F1_TILE = 8
F2_TILE = 8
RET_CHUNK = 256
NA_ROWS_PER_STEP = 8


def _params(sem):
    return pltpu.CompilerParams(dimension_semantics=sem, vmem_limit_bytes=VMEM_LIMIT_BYTES)


def _norm_mod(x, g, scale, shift):
    ms = jnp.mean(x * x, axis=-1, keepdims=True)
    return (x * lax.rsqrt(ms + EPS) * g) * (1.0 + scale) + shift


def _norm_rows(x_ref, dst_ref, dst_off, g, scale, shift, keep=None):
    rows = x_ref.shape[0]
    ch = min(rows, NORM_CHUNK)

    def body(c, carry):
        r0 = pl.multiple_of(c * ch, ch)
        h = _norm_mod(x_ref[pl.ds(r0, ch), :], g, scale, shift)
        if keep is not None:
            h = jnp.where(keep, h, 0.0)
        dst_ref[pl.ds(dst_off + r0, ch), :] = h.astype(BF16)
        return carry

    lax.fori_loop(0, rows // ch, body, 0, unroll=min(NORM_UNROLL, rows // ch))


def _col_tiles(w, tn):
    depth, k, n = w.shape
    return w.astype(BF16).reshape(depth, k, n // tn, tn).transpose(0, 2, 1, 3)


def _ada_kernel(c_ref, w_ref, b_ref, o_ref):
    c = c_ref[...]
    s = (c * jax.nn.sigmoid(c)).astype(BF16)
    o_ref[0] = jnp.dot(s, w_ref[0].astype(BF16), preferred_element_type=F32) + b_ref[0]


def _ada(c8, w_ada, b_ada):
    depth, d, n = w_ada.shape
    tn = 1024
    return pl.pallas_call(
        _ada_kernel,
        out_shape=jax.ShapeDtypeStruct((depth, 8, n), F32),
        grid=(depth, n // tn),
        in_specs=[
            pl.BlockSpec((8, d), lambda l, j: (0, 0)),
            pl.BlockSpec((1, d, tn), lambda l, j: (l, 0, j)),
            pl.BlockSpec((1, 1, tn), lambda l, j: (l, 0, j)),
        ],
        out_specs=pl.BlockSpec((1, 8, tn), lambda l, j: (l, 0, j)),
        compiler_params=_params(("arbitrary", "arbitrary")),
        name="ada",
    )(c8, w_ada, b_ada.reshape(depth, 1, n))


def _rope_tile(acc, cos, sin, scale):
    parts = []
    for hh in range(acc.shape[1] // RET_HEAD_DIM):
        ch = acc[:, hh * RET_HEAD_DIM:(hh + 1) * RET_HEAD_DIM]
        parts.append((ch * cos + pltpu.roll(ch, RET_HEAD_DIM // 2, 1) * sin) * scale)
    return jnp.concatenate(parts, axis=1)


def _in_kernel(x_ref, g_ref, sc_ref, sh_ref, cos_ref, sin_ref, w_ref,
               pool_ref, four_ref, ret_ref, na_ref, h_ref):
    j = pl.program_id(1)

    @pl.when(j == 0)
    def _():
        _norm_rows(x_ref, h_ref, 0, g_ref[0], sc_ref[0], sh_ref[0])

    def proj():
        return jnp.dot(h_ref[...], w_ref[0, 0], preferred_element_type=F32)

    @pl.when(j == 0)
    def _():
        pool_ref[...] = proj()

    @pl.when(j == 1)
    def _():
        four_ref[...] = proj().astype(BF16)

    @pl.when(j == 2)
    def _():
        ret_ref[...] = _rope_tile(proj(), cos_ref[...], sin_ref[...], 1.0).astype(BF16)

    @pl.when(j == 3)
    def _():
        ret_ref[...] = _rope_tile(proj(), cos_ref[...], sin_ref[...], RET_HEAD_DIM ** -0.5).astype(BF16)

    @pl.when((j == 4) | (j == 5))
    def _():
        ret_ref[...] = proj().astype(BF16)

    @pl.when(j == 6)
    def _():
        na_ref[...] = (proj() * (NA_HEAD_DIM ** -0.5)).astype(BF16)

    @pl.when(j >= 7)
    def _():
        na_ref[...] = proj().astype(BF16)


def _in_proj(x2, layer, g, mods, cos_t, sin_t, w_in_t, seq):
    rows, d = x2.shape
    tps = seq // TM
    nj = IN_WIDTH // TN_IN

    def mod_spec(k):
        return pl.BlockSpec((1, 1, d), lambda i, j: ((i // tps) * 6 + k, 0, 0))

    return pl.pallas_call(
        _in_kernel,
        out_shape=(
            jax.ShapeDtypeStruct((rows, BRANCH_WIDTH), F32),
            jax.ShapeDtypeStruct((rows, BRANCH_WIDTH), BF16),
            jax.ShapeDtypeStruct((rows, 4 * BRANCH_WIDTH), BF16),
            jax.ShapeDtypeStruct((rows, 3 * BRANCH_WIDTH), BF16),
            jax.ShapeDtypeStruct((rows, d), BF16),
        ),
        grid=(rows // TM, nj),
        in_specs=[
            pl.BlockSpec((TM, d), lambda i, j: (i, 0)),
            pl.BlockSpec((1, 1, d), lambda i, j: (layer, 0, 0)),
            mod_spec(1),
            mod_spec(0),
            pl.BlockSpec((TM, RET_HEAD_DIM), lambda i, j: (i % tps, 0)),
            pl.BlockSpec((TM, RET_HEAD_DIM), lambda i, j: (i % tps, 0)),
            pl.BlockSpec((1, 1, d, TN_IN), lambda i, j: (layer, j, 0, 0)),
        ],
        out_specs=(
            pl.BlockSpec((TM, TN_IN), lambda i, j: (i, 0)),
            pl.BlockSpec((TM, TN_IN), lambda i, j: (i, 0)),
            pl.BlockSpec((TM, TN_IN), lambda i, j: (i, jnp.clip(j - 2, 0, 3))),
            pl.BlockSpec((TM, TN_IN), lambda i, j: (i, jnp.clip(j - 6, 0, 2))),
            pl.BlockSpec((TM, d), lambda i, j: (i, 0)),
        ),
        compiler_params=_params(("arbitrary", "arbitrary")),
        name="in_proj",
    )(x2, g, mods, mods, cos_t, sin_t, w_in_t)


def _pool_kernel(a_ref, ap_ref, an_ref, w_ref, s_ref, o_ref, ext_ref, *, tps, seq):
    tp = a_ref.shape[0]
    ti = pl.program_id(0) % tps
    ext_ref[0:POOL_HALO, :] = jnp.where(ti == 0, 0.0, ap_ref[...])
    ext_ref[POOL_HALO:POOL_HALO + tp, :] = a_ref[...]
    ext_ref[POOL_HALO + tp:2 * POOL_HALO + tp, :] = jnp.where(ti == tps - 1, 0.0, an_ref[...])
    t = ti * tp + lax.broadcasted_iota(jnp.int32, (tp, 1), 0)
    for gi, w in enumerate(POOL_WINDOWS):
        cols = slice(gi * POOL_GROUP, (gi + 1) * POOL_GROUP)
        half = w // 2
        acc = ext_ref[pl.ds(POOL_HALO - half, tp), cols]
        for k in range(-half + 1, half):
            acc = acc + ext_ref[pl.ds(POOL_HALO + k, tp), cols]
        count = (jnp.minimum(t + half, seq) - jnp.maximum(t - half, 0)).astype(F32)
        pooled = acc / count - a_ref[:, cols]
        y = jnp.dot(pooled.astype(BF16), w_ref[0, gi], preferred_element_type=F32) * s_ref[0, :, cols]
        o_ref[:, cols] = y.astype(BF16)


def _pool_branch(a, layer, pool_w, pool_scale, seq):
    rows, w = a.shape
    tp = POOL_TILE
    tps = seq // tp
    hb = tp // POOL_HALO
    nhalo = rows // POOL_HALO
    return pl.pallas_call(
        functools.partial(_pool_kernel, tps=tps, seq=seq),
        out_shape=jax.ShapeDtypeStruct((rows, w), BF16),
        grid=(rows // tp,),
        in_specs=[
            pl.BlockSpec((tp, w), lambda i: (i, 0)),
            pl.BlockSpec((POOL_HALO, w), lambda i: (jnp.maximum(i * hb - 1, 0), 0)),
            pl.BlockSpec((POOL_HALO, w), lambda i: (jnp.minimum((i + 1) * hb, nhalo - 1), 0)),
            pl.BlockSpec((1, len(POOL_WINDOWS), POOL_GROUP, POOL_GROUP), lambda i: (layer, 0, 0, 0)),
            pl.BlockSpec((1, 1, w), lambda i: (layer, 0, 0)),
        ],
        out_specs=pl.BlockSpec((tp, w), lambda i: (i, 0)),
        scratch_shapes=[pltpu.VMEM((tp + 2 * POOL_HALO, w), F32)],
        compiler_params=_params(("arbitrary",)),
        name="pool",
    )(a, a, a, pool_w, pool_scale)


def _f1_kernel(x_ref, m1_ref, tc_ref, ts_ref, yr_ref, yi_ref):
    n1 = x_ref.shape[1]
    y = jnp.dot(m1_ref[...], x_ref[0], preferred_element_type=F32)
    yr = y[:n1]
    yi = y[n1:]
    c = tc_ref[...]
    s = ts_ref[...]
    yr_ref[0] = (yr * c + yi * s).astype(BF16)
    yi_ref[0] = (yi * c - yr * s).astype(BF16)


def _f2_kernel(yr_ref, yi_ref, ma_ref, mb_ref, cc_ref, cs_ref, o_ref, zr_ref, zi_ref, *, scale):
    n2 = F_T2
    w = o_ref.shape[2] // F2_TILE
    for s in range(F2_TILE):
        rows = slice(s * n2, (s + 1) * n2)
        z = (jnp.dot(ma_ref[...], yr_ref[0, rows, :], preferred_element_type=F32)
             + jnp.dot(mb_ref[...], yi_ref[0, rows, :], preferred_element_type=F32))
        zr_ref[rows, :] = z[:n2].astype(BF16)
        zi_ref[rows, :] = z[n2:].astype(BF16)
    o = (jnp.dot(zr_ref[...], cc_ref[...], preferred_element_type=F32)
         + jnp.dot(zi_ref[...], cs_ref[...], preferred_element_type=F32)) * scale
    for s in range(F2_TILE):
        o_ref[0, :, s * w:(s + 1) * w] = o[s * n2:(s + 1) * n2].astype(BF16)


def _dft_cos_sin(n):
    k = np.arange(n)
    ang = 2.0 * np.pi * ((k[:, None] * k[None, :]) % n) / n
    return np.cos(ang), np.sin(ang)


def _fourier_consts(seq):
    n2 = F_T2
    n1 = seq // n2
    c1, s1 = _dft_cos_sin(n1)
    c2, s2 = _dft_cos_sin(n2)
    cg, sg = _dft_cos_sin(FOURIER_GROUP)
    eye = np.eye(BRANCH_WIDTH // FOURIER_GROUP)
    m1 = jnp.asarray(np.concatenate([c1, -s1], axis=0), BF16)
    ma = jnp.asarray(np.concatenate([c2, -s2], axis=0), BF16)
    mb = jnp.asarray(np.concatenate([s2, c2], axis=0), BF16)
    cc = jnp.asarray(np.kron(eye, cg), BF16)
    cs = jnp.asarray(np.kron(eye, sg), BF16)
    prod = (jnp.arange(n1, dtype=jnp.int32)[:, None] * jnp.arange(n2, dtype=jnp.int32)[None, :]) % seq
    ang = prod.astype(F32) * (2.0 * math.pi / seq)
    shape = (n1, n2, BRANCH_WIDTH)
    tc = jnp.broadcast_to(jnp.cos(ang)[:, :, None], shape).reshape(n1, n2 * BRANCH_WIDTH)
    ts = jnp.broadcast_to(jnp.sin(ang)[:, :, None], shape).reshape(n1, n2 * BRANCH_WIDTH)
    return dict(m1=m1, ma=ma, mb=mb, cc=cc, cs=cs, tc=tc, ts=ts)


def _fourier_branch(f, fc, batch, seq):
    w = BRANCH_WIDTH
    n2 = F_T2
    n1 = seq // n2
    blk = F1_TILE * w
    x3 = f.reshape(batch, n1, n2 * w)
    yr, yi = pl.pallas_call(
        _f1_kernel,
        out_shape=(jax.ShapeDtypeStruct((batch, n1, n2 * w), BF16),) * 2,
        grid=(n2 // F1_TILE, batch),
        in_specs=[
            pl.BlockSpec((1, n1, blk), lambda j, b: (b, 0, j)),
            pl.BlockSpec((2 * n1, n1), lambda j, b: (0, 0)),
            pl.BlockSpec((n1, blk), lambda j, b: (0, j)),
            pl.BlockSpec((n1, blk), lambda j, b: (0, j)),
        ],
        out_specs=(pl.BlockSpec((1, n1, blk), lambda j, b: (b, 0, j)),) * 2,
        compiler_params=_params(("arbitrary", "arbitrary")),
        name="fourier1",
    )(x3, fc["m1"], fc["tc"], fc["ts"])
    yr = yr.reshape(batch, n1 * n2, w)
    yi = yi.reshape(batch, n1 * n2, w)
    rows = F2_TILE * n2
    out = pl.pallas_call(
        functools.partial(_f2_kernel, scale=1.0 / math.sqrt(seq * FOURIER_GROUP)),
        out_shape=jax.ShapeDtypeStruct((batch, n2, n1 * w), BF16),
        grid=(batch, n1 // F2_TILE),
        in_specs=[
            pl.BlockSpec((1, rows, w), lambda b, j: (b, j, 0)),
            pl.BlockSpec((1, rows, w), lambda b, j: (b, j, 0)),
            pl.BlockSpec((2 * n2, n2), lambda b, j: (0, 0)),
            pl.BlockSpec((2 * n2, n2), lambda b, j: (0, 0)),
            pl.BlockSpec((w, w), lambda b, j: (0, 0)),
            pl.BlockSpec((w, w), lambda b, j: (0, 0)),
        ],
        out_specs=pl.BlockSpec((1, n2, F2_TILE * w), lambda b, j: (b, 0, j)),
        scratch_shapes=[pltpu.VMEM((rows, w), BF16), pltpu.VMEM((rows, w), BF16)],
        compiler_params=_params(("arbitrary", "arbitrary")),
        name="fourier2",
    )(yr, yi, fc["ma"], fc["mb"], fc["cc"], fc["cs"])
    return out.reshape(batch * seq, w)


def _log_sigmoid(x):
    return jnp.minimum(x, 0.0) - jnp.log1p(jnp.exp(-jnp.abs(x)))


def _ret_kernel(q_ref, k_ref, v_ref, g_ref, df_ref, db_ref, o_ref, kv_ref, r_ref, *, seq):
    c = RET_CHUNK
    d = RET_HEAD_DIM
    nc = seq // c
    lgf = _log_sigmoid(df_ref[0, 0])
    lgb = _log_sigmoid(db_ref[0, 0])
    ii = lax.broadcasted_iota(jnp.int32, (c, 1), 0).astype(F32)
    kdf = jnp.exp(lgf * (c - 1.0 - ii))
    kdb = jnp.exp(lgb * ii)
    qdf = jnp.exp(lgf * (ii + 1.0))
    qdb = jnp.exp(lgb * (c - ii))
    diff = (lax.broadcasted_iota(jnp.int32, (c, c), 0)
            - lax.broadcasted_iota(jnp.int32, (c, c), 1)).astype(F32)
    lgf_c = jnp.concatenate([lgf] * (c // d), axis=1)
    lgb_c = jnp.concatenate([lgb] * (c // d), axis=1)
    dmat = jnp.where(diff >= 0.0, jnp.exp(lgf_c * jnp.maximum(diff, 0.0)),
                     jnp.exp(lgb_c * jnp.maximum(-diff, 0.0)))
    chunk_f = jnp.exp(lgf * float(c))
    chunk_b = jnp.exp(lgb * float(c))

    def kv_body(n, carry):
        rows = pl.ds(pl.multiple_of(n * c, c), c)
        kc = k_ref[0, rows, :].astype(F32)
        kd = jnp.concatenate([kc * kdf, kc * kdb], axis=1).astype(BF16)
        kv_ref[n] = lax.dot_general(kd, v_ref[0, rows, :], (((0,), (0,)), ((), ())),
                                    preferred_element_type=F32)
        return carry

    lax.fori_loop(0, nc, kv_body, 0, unroll=2)

    def fwd_body(n, state):
        r_ref[n, 0:d, :] = state.astype(BF16)
        return chunk_f * state + kv_ref[n, 0:d, :]

    lax.fori_loop(0, nc, fwd_body, jnp.zeros((d, d), F32))

    def bwd_body(m, state):
        n = nc - 1 - m
        r_ref[n, d:2 * d, :] = state.astype(BF16)
        return chunk_b * state + kv_ref[n, d:2 * d, :]

    lax.fori_loop(0, nc, bwd_body, jnp.zeros((d, d), F32))

    def out_body(t, carry):
        staged = []
        for n in (2 * t, 2 * t + 1):
            rows = pl.ds(pl.multiple_of(n * c, c), c)
            qb = q_ref[0, rows, :]
            s = lax.dot_general(qb, k_ref[0, rows, :], (((1,), (1,)), ((), ())),
                                preferred_element_type=F32)
            qc = qb.astype(F32)
            qd = jnp.concatenate([qc * qdf, qc * qdb], axis=1).astype(BF16)
            staged.append((rows, s, jnp.dot(qd, r_ref[n], preferred_element_type=F32)))
        for rows, s, cross in staged:
            o = jnp.dot((s * dmat).astype(BF16), v_ref[0, rows, :], preferred_element_type=F32) + cross
            mu = jnp.mean(o, axis=-1, keepdims=True)
            var = jnp.mean(jnp.square(o - mu), axis=-1, keepdims=True)
            on = (o - mu) * lax.rsqrt(var + EPS)
            g = g_ref[0, rows, :].astype(F32)
            o_ref[0, rows, :] = (on * (g * jax.nn.sigmoid(g))).astype(BF16)
        return carry

    lax.fori_loop(0, nc // 2, out_body, 0)


def _ret_branch(ret, layer, decay_f, decay_b, batch, seq):
    d = RET_HEAD_DIM
    nc = seq // RET_CHUNK
    r3 = ret.reshape(batch, seq, 4 * BRANCH_WIDTH)

    def sec(k):
        return pl.BlockSpec((1, seq, d), lambda b, h: (b, 0, k * RET_HEADS + h))

    decay_spec = pl.BlockSpec((1, 1, 1, d), lambda b, h: (layer, h, 0, 0))
    out = pl.pallas_call(
        functools.partial(_ret_kernel, seq=seq),
        out_shape=jax.ShapeDtypeStruct((batch, seq, BRANCH_WIDTH), BF16),
        grid=(batch, RET_HEADS),
        in_specs=[sec(0), sec(1), sec(2), sec(3), decay_spec, decay_spec],
        out_specs=pl.BlockSpec((1, seq, d), lambda b, h: (b, 0, h)),
        scratch_shapes=[pltpu.VMEM((nc, 2 * d, d), F32), pltpu.VMEM((nc, 2 * d, d), BF16)],
        compiler_params=_params(("arbitrary", "arbitrary")),
        name="retention",
    )(r3, r3, r3, r3, decay_f, decay_b)
    return out.reshape(batch * seq, BRANCH_WIDTH)


def _na_kernel(q_ref, k0_ref, k1_ref, k2_ref, v0_ref, v1_ref, v2_ref, bias_ref, o_ref,
               kw_ref, vw_ref, *, grid_rows):
    rr = NA_ROWS_PER_STEP
    gw = GRID_W
    blk = rr * gw
    kr = NA_ROWS_MAX
    m = pl.program_id(1)
    for idx, (kref, vref) in enumerate(((k0_ref, v0_ref), (k1_ref, v1_ref), (k2_ref, v2_ref))):
        kw_ref[idx * blk:(idx + 1) * blk, :] = kref[0]
        vw_ref[idx * blk:(idx + 1) * blk, :] = vref[0]
    lane = lax.broadcasted_iota(jnp.int32, (gw, 2 * NA_HEAD_DIM), 1)
    low = lane < NA_HEAD_DIM

    pair_cols = [slice(jp * 2 * NA_HEAD_DIM, (jp + 1) * 2 * NA_HEAD_DIM) for jp in range(NA_HEADS // 2)]

    def window(r):
        r_glob = m * rr + r
        rs = jnp.clip(r_glob - kr // 2, 0, grid_rows - kr)
        krows = pl.ds(pl.multiple_of((rs - (m - 1) * rr) * gw, gw), kr * gw)
        qrows = pl.ds(pl.multiple_of(r * gw, gw), gw)
        return qrows, krows, r_glob - rs

    def scores(r):
        qrows, krows, didx = window(r)
        out = []
        for jp, cols in enumerate(pair_cols):
            q2 = q_ref[0, qrows, cols]
            k2 = kw_ref[krows, cols]
            for e in range(2):
                qm = jnp.where(low if e == 0 else jnp.logical_not(low), q2, jnp.zeros_like(q2))
                s = lax.dot_general(qm, k2, (((1,), (1,)), ((), ())), preferred_element_type=F32)
                out.append(s + bias_ref[0, didx, 2 * jp + e])
        return out

    def softmax(ss):
        out = []
        for s in ss:
            p = jnp.exp(s - jnp.max(s, axis=-1, keepdims=True))
            out.append((p.astype(BF16), jnp.sum(p, axis=-1, keepdims=True)))
        return out

    def values(r, ps):
        qrows, krows, _ = window(r)
        for jp, cols in enumerate(pair_cols):
            v2 = vw_ref[krows, cols]
            outs = [jnp.dot(p, v2, preferred_element_type=F32) / l for p, l in ps[2 * jp:2 * jp + 2]]
            o_ref[0, qrows, cols] = jnp.where(low, outs[0], outs[1]).astype(BF16)

    def pair_body(t, carry):
        s0 = scores(2 * t)
        s1 = scores(2 * t + 1)
        values(2 * t, softmax(s0))
        values(2 * t + 1, softmax(s1))
        return carry

    lax.fori_loop(0, rr // 2, pair_body, 0)


def _na_bias(rpb):
    rpb = rpb.astype(F32)
    per_q = []
    for qc in range(GRID_W):
        cs = min(max(qc - NA_COLS // 2, 0), GRID_W - NA_COLS)
        lo = (NA_COLS - 1) - qc + cs
        seg = rpb[..., lo:lo + NA_COLS]
        per_q.append(jnp.pad(seg, ((0, 0),) * 3 + ((cs, GRID_W - NA_COLS - cs),), constant_values=NEG_INF))
    toep = jnp.stack(per_q, axis=3)
    per_dl = []
    for dl in range(NA_ROWS_MAX):
        lo = NA_ROWS_MAX - 1 - dl
        win = toep[:, :, lo:lo + NA_ROWS_MAX]
        per_dl.append(jnp.transpose(win, (0, 1, 3, 2, 4)).reshape(
            rpb.shape[0], NA_HEADS, GRID_W, NA_ROWS_MAX * GRID_W))
    return jnp.stack(per_dl, axis=1)


def _na_branch(na, layer, bias, batch, seq):
    w = BRANCH_WIDTH
    grid_rows = seq // GRID_W
    rr = NA_ROWS_PER_STEP
    blk = rr * GRID_W
    nblk = grid_rows // rr
    n3 = na.reshape(batch, seq, 3 * w)

    def kv_spec(col, shift):
        return pl.BlockSpec((1, blk, w), lambda b, m: (b, jnp.clip(m + shift, 0, nblk - 1), col))

    out = pl.pallas_call(
        functools.partial(_na_kernel, grid_rows=grid_rows),
        out_shape=jax.ShapeDtypeStruct((batch, seq, w), BF16),
        grid=(batch, nblk),
        in_specs=[pl.BlockSpec((1, blk, w), lambda b, m: (b, m, 0)),
                  kv_spec(1, -1), kv_spec(1, 0), kv_spec(1, 1),
                  kv_spec(2, -1), kv_spec(2, 0), kv_spec(2, 1),
                  pl.BlockSpec((1,) + bias.shape[1:], lambda b, m: (layer, 0, 0, 0, 0))],
        out_specs=pl.BlockSpec((1, blk, w), lambda b, m: (b, m, 0)),
        scratch_shapes=[pltpu.VMEM((3 * blk, w), BF16), pltpu.VMEM((3 * blk, w), BF16)],
        compiler_params=_params(("arbitrary", "arbitrary")),
        name="nbr_attn",
    )(n3, n3, n3, n3, n3, n3, n3, bias)
    return out.reshape(batch * seq, w)


def _merge_kernel(h_ref, gate_ref, xs_ref,
                  br0, br1, br2, br3, wg0, wg1, wg2, wg3, bg0, bg1, bg2, bg3,
                  wb0, wb1, wb2, wb3, wo_ref, o_ref, m_ref, *, nj):
    j = pl.program_id(1)

    @pl.when(j < nj)
    def _():
        acc = None
        for br, wg, bg, wb in ((br0, wg0, bg0, wb0), (br1, wg1, bg1, wb1),
                               (br2, wg2, bg2, wb2), (br3, wg3, bg3, wb3)):
            gate = jax.nn.sigmoid(jnp.dot(h_ref[...], wg[0, 0], preferred_element_type=F32) + bg[0])
            term = gate * jnp.dot(br[...], wb[0, 0, 0], preferred_element_type=F32)
            acc = term if acc is None else acc + term
        m_ref[j] = acc.astype(BF16)

    @pl.when(j >= nj)
    def _():
        tn = wo_ref.shape[3]
        out = jnp.dot(m_ref[0], wo_ref[0, 0, 0:tn, :], preferred_element_type=F32)
        for kk in range(1, nj):
            out = out + jnp.dot(m_ref[kk], wo_ref[0, 0, kk * tn:(kk + 1) * tn, :], preferred_element_type=F32)
        o_ref[...] = xs_ref[...] + gate_ref[0] * out


def _merge(x2, h, layer, mods, branches, w_gate_t, b_gate, w_branch_t, w_out_t, seq):
    rows, d = x2.shape
    tn = TN_MERGE
    nj = d // tn
    tps = seq // TM

    def first(j):
        return jnp.minimum(j, nj - 1)

    def second(j):
        return jnp.maximum(j - nj, 0)

    def gate_w_spec(b):
        return pl.BlockSpec((1, 1, d, tn), lambda i, j: (layer, b * nj + first(j), 0, 0))

    def gate_b_spec(b):
        return pl.BlockSpec((1, 1, tn), lambda i, j: (layer, 0, b * nj + first(j)))

    def branch_w_spec(b):
        return pl.BlockSpec((1, 1, 1, BRANCH_WIDTH, tn), lambda i, j: (layer, b, first(j), 0, 0))

    br_spec = pl.BlockSpec((TM, BRANCH_WIDTH), lambda i, j: (i, 0))
    in_specs = [
        pl.BlockSpec((TM, d), lambda i, j: (i, 0)),
        pl.BlockSpec((1, 1, tn), lambda i, j: ((i // tps) * 6 + 2, 0, second(j))),
        pl.BlockSpec((TM, tn), lambda i, j: (i, second(j))),
        br_spec, br_spec, br_spec, br_spec,
        gate_w_spec(0), gate_w_spec(1), gate_w_spec(2), gate_w_spec(3),
        gate_b_spec(0), gate_b_spec(1), gate_b_spec(2), gate_b_spec(3),
        branch_w_spec(0), branch_w_spec(1), branch_w_spec(2), branch_w_spec(3),
        pl.BlockSpec((1, 1, d, tn), lambda i, j: (layer, second(j), 0, 0)),
    ]
    return pl.pallas_call(
        functools.partial(_merge_kernel, nj=nj),
        out_shape=jax.ShapeDtypeStruct((rows, d), F32),
        grid=(rows // TM, 2 * nj),
        in_specs=in_specs,
        out_specs=pl.BlockSpec((TM, tn), lambda i, j: (i, second(j))),
        scratch_shapes=[pltpu.VMEM((nj, TM, tn), BF16)],
        compiler_params=_params(("arbitrary", "arbitrary")),
        name="merge",
    )(h, mods, x2, *branches,
      w_gate_t, w_gate_t, w_gate_t, w_gate_t, b_gate, b_gate, b_gate, b_gate,
      w_branch_t, w_branch_t, w_branch_t, w_branch_t, w_out_t)


def _ffn_kernel(x_ref, xp_ref, xn_ref, g_ref, sc_ref, sh_ref, gate_ref, xs_ref,
                wa_ref, wb_ref, cwa_ref, cwb_ref, cba_ref, cbb_ref, wd_ref,
                o_ref, h_ref, ua0_ref, ub0_ref, ua1_ref, ub1_ref, act_ref, *, tps, nf):
    i = pl.program_id(0)
    j = pl.program_id(1)
    tm = x_ref.shape[0]
    tf = ua0_ref.shape[1]
    ti = i % tps

    ch = FFN_CHUNK
    ext = ch + 2 * 8

    bufs = ((ua0_ref, ub0_ref), (ua1_ref, ub1_ref))

    def up(slot):
        h = h_ref[...]
        bufs[slot][0][...] = jnp.dot(h, wa_ref[0, 0], preferred_element_type=F32)
        bufs[slot][1][...] = jnp.dot(h, wb_ref[0, 0], preferred_element_type=F32)

    def conv(u_ref, cw_ref, cb_ref, r0):
        u = u_ref[r0 + HALO - 8:r0 + HALO - 8 + ext, :]
        prev = pltpu.roll(u, 1, 0)[8:8 + ch]
        nxt = pltpu.roll(u, ext - 1, 0)[8:8 + ch]
        return (prev * cw_ref[0, 0:1, :] + u[8:8 + ch] * cw_ref[0, 1:2, :]
                + nxt * cw_ref[0, 2:3, :] + cb_ref[0])

    def activate(slot, tile):
        for r0 in range(0, tm, ch):
            a = conv(bufs[slot][0], cwa_ref, cba_ref, r0)
            b = conv(bufs[slot][1], cwb_ref, cbb_ref, r0)
            act_ref[tile, r0:r0 + ch, :] = (jax.nn.gelu(a, approximate=True) * b).astype(BF16)

    def down():
        out = jnp.dot(act_ref[0], wd_ref[0, 0, 0:tf, :], preferred_element_type=F32)
        for kk in range(1, nf):
            out = out + jnp.dot(act_ref[kk], wd_ref[0, 0, kk * tf:(kk + 1) * tf, :],
                                preferred_element_type=F32)
        o_ref[...] = xs_ref[...] + gate_ref[0] * out

    @pl.when(j == 0)
    def _():
        g = g_ref[0]
        sc = sc_ref[0]
        sh = sh_ref[0]
        _norm_rows(xp_ref, h_ref, 0, g, sc, sh, keep=ti != 0)
        _norm_rows(x_ref, h_ref, HALO, g, sc, sh)
        _norm_rows(xn_ref, h_ref, HALO + tm, g, sc, sh, keep=ti != tps - 1)
        up(0)

    for parity in range(2):
        @pl.when((j >= 1) & (j < nf) & (j % 2 == parity))
        def _():
            activate(1 - parity, j - 1)
            up(parity)

    @pl.when(j == nf)
    def _():
        activate((nf - 1) % 2, nf - 1)
        down()

    @pl.when(j > nf)
    def _():
        down()


def _ffn(x2, layer, g, mods, w_up_t, conv_w, conv_b, w_down_t, seq):
    rows, d = x2.shape
    tm = TM_FFN
    tps = seq // tm
    nf = D_FF // TF
    tn = TN_DOWN
    nd = d // tn
    hb = tm // HALO
    nhalo = rows // HALO

    def first(j):
        return jnp.minimum(j, nf - 1)

    def second(j):
        return jnp.maximum(j - nf, 0)

    def lagged(j):
        return jnp.clip(j - 1, 0, nf - 1)

    return pl.pallas_call(
        functools.partial(_ffn_kernel, tps=tps, nf=nf),
        out_shape=jax.ShapeDtypeStruct((rows, d), F32),
        grid=(rows // tm, nf + nd),
        in_specs=[
            pl.BlockSpec((tm, d), lambda i, j: (i, 0), pipeline_mode=pl.Buffered(1)),
            pl.BlockSpec((HALO, d), lambda i, j: (jnp.maximum(i * hb - 1, 0), 0)),
            pl.BlockSpec((HALO, d), lambda i, j: (jnp.minimum((i + 1) * hb, nhalo - 1), 0)),
            pl.BlockSpec((1, 1, d), lambda i, j: (layer, 0, 0)),
            pl.BlockSpec((1, 1, d), lambda i, j: ((i // tps) * 6 + 4, 0, 0)),
            pl.BlockSpec((1, 1, d), lambda i, j: ((i // tps) * 6 + 3, 0, 0)),
            pl.BlockSpec((1, 1, tn), lambda i, j: ((i // tps) * 6 + 5, 0, second(j))),
            pl.BlockSpec((tm, tn), lambda i, j: (i, second(j))),
            pl.BlockSpec((1, 1, d, TF), lambda i, j: (layer, first(j), 0, 0)),
            pl.BlockSpec((1, 1, d, TF), lambda i, j: (layer, nf + first(j), 0, 0)),
            pl.BlockSpec((1, 3, TF), lambda i, j: (layer, 0, lagged(j))),
            pl.BlockSpec((1, 3, TF), lambda i, j: (layer, 0, nf + lagged(j))),
            pl.BlockSpec((1, 1, TF), lambda i, j: (layer, 0, lagged(j))),
            pl.BlockSpec((1, 1, TF), lambda i, j: (layer, 0, nf + lagged(j))),
            pl.BlockSpec((1, 1, D_FF, tn), lambda i, j: (layer, second(j), 0, 0)),
        ],
        out_specs=pl.BlockSpec((tm, tn), lambda i, j: (i, second(j))),
        scratch_shapes=[pltpu.VMEM((tm + 2 * HALO, d), BF16),
                        pltpu.VMEM((tm + 2 * HALO, TF), F32),
                        pltpu.VMEM((tm + 2 * HALO, TF), F32),
                        pltpu.VMEM((tm + 2 * HALO, TF), F32),
                        pltpu.VMEM((tm + 2 * HALO, TF), F32),
                        pltpu.VMEM((nf, tm, TF), BF16)],
        compiler_params=_params(("arbitrary", "arbitrary")),
        name="conv_ffn",
    )(x2, x2, x2, g, mods, mods, mods, x2, w_up_t, w_up_t, conv_w, conv_w, conv_b, conv_b, w_down_t)


def _final_kernel(x_ref, g_ref, o_ref):
    x = x_ref[...]
    ms = jnp.mean(x * x, axis=-1, keepdims=True)
    o_ref[...] = x * lax.rsqrt(ms + EPS) * g_ref[...]


def _final_norm(x2, g_final):
    rows, d = x2.shape
    tr = 256
    return pl.pallas_call(
        _final_kernel,
        out_shape=jax.ShapeDtypeStruct((rows, d), F32),
        grid=(rows // tr,),
        in_specs=[pl.BlockSpec((tr, d), lambda i: (i, 0)), pl.BlockSpec((1, d), lambda i: (0, 0))],
        out_specs=pl.BlockSpec((tr, d), lambda i: (i, 0)),
        compiler_params=_params(("arbitrary",)),
        name="final_norm",
    )(x2, g_final)


def _rope_tables(seq):
    half = RET_HEAD_DIM // 2
    inv = ROPE_BASE ** (-jnp.arange(half, dtype=F32) / half)
    ang = jnp.arange(seq, dtype=F32)[:, None] * inv[None, :]
    cos = jnp.cos(ang)
    sin = jnp.sin(ang)
    return jnp.concatenate([cos, cos], axis=1), jnp.concatenate([-sin, sin], axis=1)


def _trunk(x, mods, wts):
    batch, seq, d = x.shape
    x2 = x.reshape(batch * seq, d)
    cos_t, sin_t = _rope_tables(seq)
    fc = _fourier_consts(seq)
    for l in range(DEPTH):
        m = mods[l]
        pool_in, four_in, ret_in, na_in, h = _in_proj(x2, l, wts["g_mix"], m, cos_t, sin_t, wts["w_in"], seq)
        branches = (
            _pool_branch(pool_in, l, wts["pool_w"], wts["pool_scale"], seq),
            _fourier_branch(four_in, fc, batch, seq),
            _ret_branch(ret_in, l, wts["ret_decay_f"], wts["ret_decay_b"], batch, seq),
            _na_branch(na_in, l, wts["na_bias"], batch, seq),
        )
        x2 = _merge(x2, h, l, m, branches, wts["w_gate"], wts["b_gate"], wts["w_branch"], wts["w_out"], seq)
        x2 = _ffn(x2, l, wts["g_ffn"], m, wts["w_up"], wts["conv_w"], wts["conv_b"], wts["w_down"], seq)
    return _final_norm(x2, wts["g_final"]).reshape(batch, seq, d)


def kernel(x_prompt, x_sample, c_prompt, c_sample, w_ada, b_ada, g_mix, w_in, pool_w, pool_scale,
           ret_decay_f, ret_decay_b, na_rpb, w_branch, w_gate, b_gate, w_out, g_ffn, w_up, conv_w,
           conv_b, w_down, g_final):
    d = D_MODEL
    nb_p = c_prompt.shape[0]
    nb_s = c_sample.shape[0]
    c8 = jnp.concatenate([c_prompt, c_sample, jnp.zeros((8 - nb_p - nb_s, d), F32)], axis=0)
    mod_all = _ada(c8, w_ada, b_ada)
    mods_p = mod_all[:, :nb_p].reshape(DEPTH, nb_p * 6, 1, d)
    mods_s = mod_all[:, nb_p:nb_p + nb_s].reshape(DEPTH, nb_s * 6, 1, d)
    nj = d // TN_MERGE
    decay_shape = (DEPTH, RET_HEADS, 1, RET_HEAD_DIM)
    wts = dict(
        g_mix=g_mix.reshape(DEPTH, 1, d),
        w_in=_col_tiles(w_in, TN_IN),
        pool_w=pool_w.astype(BF16),
        pool_scale=pool_scale.reshape(DEPTH, 1, BRANCH_WIDTH),
        ret_decay_f=jnp.broadcast_to(ret_decay_f.astype(F32)[:, :, None, None], decay_shape),
        ret_decay_b=jnp.broadcast_to(ret_decay_b.astype(F32)[:, :, None, None], decay_shape),
        na_bias=_na_bias(na_rpb),
        w_branch=w_branch.astype(BF16).reshape(DEPTH, N_BRANCH, BRANCH_WIDTH, nj, TN_MERGE).transpose(0, 1, 3, 2, 4),
        w_gate=_col_tiles(w_gate, TN_MERGE),
        b_gate=b_gate.reshape(DEPTH, 1, N_BRANCH * d),
        w_out=_col_tiles(w_out, TN_MERGE),
        g_ffn=g_ffn.reshape(DEPTH, 1, d),
        w_up=_col_tiles(w_up, TF),
        conv_w=conv_w,
        conv_b=conv_b.reshape(DEPTH, 1, 2 * D_FF),
        w_down=_col_tiles(w_down, TN_DOWN),
        g_final=g_final.reshape(1, d),
    )
    y_prompt = _trunk(x_prompt, mods_p, wts)
    y_sample = _trunk(x_sample, mods_s, wts)
    return (y_prompt, y_sample)
```

```python
import functools
import math

import numpy as np
import jax
import jax.numpy as jnp
from jax import lax
from jax.experimental import pallas as pl
from jax.experimental.pallas import tpu as pltpu

F32 = jnp.float32
BF16 = jnp.bfloat16

D_MODEL = 2048
DEPTH = 4
GRID_W = 64
BRANCH_WIDTH = 512
N_BRANCH = 4
POOL_WINDOWS = (2, 4, 8, 16)
POOL_GROUP = 128
FOURIER_GROUP = 128
RET_HEADS = 4
RET_HEAD_DIM = 128
ROPE_BASE = 10000.0
NA_HEADS = 8
NA_HEAD_DIM = 64
NA_ROWS_MAX = 8
NA_COLS = 16
D_FF = 5632
EPS = 1e-6
NEG_INF = -1e30
IN_WIDTH = 9 * BRANCH_WIDTH

VMEM_LIMIT_BYTES = 56 * 1024 * 1024
BF16_SUBLANES = 16

TM = 1024
TN_IN = 512
TN_MERGE = 256
TM_FFN = 1024
TF = 256
TN_DOWN = 256
HALO = BF16_SUBLANES
NORM_CHUNK = 32
NORM_UNROLL = 4
FFN_CHUNK = 64
POOL_TILE = 512
POOL_HALO = 8
F_T2 = 128
F1_TILE = 8
F2_TILE = 8
RET_CHUNK = 256
NA_ROWS_PER_STEP = 8
NA_ROW_GROUP = 4


def _params(sem):
    return pltpu.CompilerParams(dimension_semantics=sem, vmem_limit_bytes=VMEM_LIMIT_BYTES)


def _norm_rows(x_ref, dst_ref, dst_off, g, scale, shift, keep=None):
    rows = x_ref.shape[0]
    ch = min(rows, NORM_CHUNK)
    gain = g * (1.0 + scale)

    def body(c, carry):
        r0 = pl.multiple_of(c * ch, ch)
        x = x_ref[pl.ds(r0, ch), :]
        ms = jnp.mean(x * x, axis=-1, keepdims=True)
        h = (x * lax.rsqrt(ms + EPS)) * gain + shift
        if keep is not None:
            h = jnp.where(keep, h, 0.0)
        dst_ref[pl.ds(dst_off + r0, ch), :] = h.astype(BF16)
        return carry

    lax.fori_loop(0, rows // ch, body, 0, unroll=min(NORM_UNROLL, rows // ch))


def _ada_kernel(c_ref, w_ref, b_ref, o_ref):
    c = c_ref[...]
    s = (c * jax.nn.sigmoid(c)).astype(BF16)
    o_ref[0] = jnp.dot(s, w_ref[0].astype(BF16), preferred_element_type=F32) + b_ref[0]


def _ada(c8, w_ada, b_ada):
    depth, d, n = w_ada.shape
    tn = 1024
    return pl.pallas_call(
        _ada_kernel,
        out_shape=jax.ShapeDtypeStruct((depth, 8, n), F32),
        grid=(depth, n // tn),
        in_specs=[
            pl.BlockSpec((8, d), lambda l, j: (0, 0)),
            pl.BlockSpec((1, d, tn), lambda l, j: (l, 0, j)),
            pl.BlockSpec((1, 1, tn), lambda l, j: (l, 0, j)),
        ],
        out_specs=pl.BlockSpec((1, 8, tn), lambda l, j: (l, 0, j)),
        compiler_params=_params(("arbitrary", "arbitrary")),
        name="ada",
    )(c8, w_ada, b_ada.reshape(depth, 1, n))


def _rope_tile(acc, cos, sin, scale):
    parts = []
    for hh in range(acc.shape[1] // RET_HEAD_DIM):
        ch = acc[:, hh * RET_HEAD_DIM:(hh + 1) * RET_HEAD_DIM]
        parts.append((ch * cos + pltpu.roll(ch, RET_HEAD_DIM // 2, 1) * sin) * scale)
    return jnp.concatenate(parts, axis=1)


def _in_kernel(x_ref, g_ref, sc_ref, sh_ref, cos_ref, sin_ref, w_ref,
               pool_ref, four_ref, ret_ref, na_ref, h_ref):
    j = pl.program_id(1)

    @pl.when(j == 0)
    def _():
        _norm_rows(x_ref, h_ref, 0, g_ref[0], sc_ref[0], sh_ref[0])

    def proj():
        return jnp.dot(h_ref[...], w_ref[0], preferred_element_type=F32)

    @pl.when(j == 0)
    def _():
        pool_ref[...] = proj()

    @pl.when(j == 1)
    def _():
        four_ref[...] = proj().astype(BF16)

    @pl.when(j == 2)
    def _():
        ret_ref[...] = _rope_tile(proj(), cos_ref[...], sin_ref[...], 1.0).astype(BF16)

    @pl.when(j == 3)
    def _():
        ret_ref[...] = _rope_tile(proj(), cos_ref[...], sin_ref[...], RET_HEAD_DIM ** -0.5).astype(BF16)

    @pl.when((j == 4) | (j == 5))
    def _():
        ret_ref[...] = proj().astype(BF16)

    @pl.when(j == 6)
    def _():
        na_ref[...] = (proj() * (NA_HEAD_DIM ** -0.5)).astype(BF16)

    @pl.when(j >= 7)
    def _():
        na_ref[...] = proj().astype(BF16)


def _in_proj(x2, layer, g, mods, cos_t, sin_t, w_in_t, seq):
    rows, d = x2.shape
    tps = seq // TM
    nj = IN_WIDTH // TN_IN

    def mod_spec(k):
        return pl.BlockSpec((1, 1, d), lambda i, j: ((i // tps) * 6 + k, 0, 0))

    return pl.pallas_call(
        _in_kernel,
        out_shape=(
            jax.ShapeDtypeStruct((rows, BRANCH_WIDTH), F32),
            jax.ShapeDtypeStruct((rows, BRANCH_WIDTH), BF16),
            jax.ShapeDtypeStruct((rows, 4 * BRANCH_WIDTH), BF16),
            jax.ShapeDtypeStruct((rows, 3 * BRANCH_WIDTH), BF16),
            jax.ShapeDtypeStruct((rows, d), BF16),
        ),
        grid=(rows // TM, nj),
        in_specs=[
            pl.BlockSpec((TM, d), lambda i, j: (i, 0)),
            pl.BlockSpec((1, 1, d), lambda i, j: (layer, 0, 0)),
            mod_spec(1),
            mod_spec(0),
            pl.BlockSpec((TM, RET_HEAD_DIM), lambda i, j: (i % tps, 0)),
            pl.BlockSpec((TM, RET_HEAD_DIM), lambda i, j: (i % tps, 0)),
            pl.BlockSpec((1, d, TN_IN), lambda i, j: (layer, 0, j)),
        ],
        out_specs=(
            pl.BlockSpec((TM, TN_IN), lambda i, j: (i, 0)),
            pl.BlockSpec((TM, TN_IN), lambda i, j: (i, 0)),
            pl.BlockSpec((TM, TN_IN), lambda i, j: (i, jnp.clip(j - 2, 0, 3))),
            pl.BlockSpec((TM, TN_IN), lambda i, j: (i, jnp.clip(j - 6, 0, 2))),
            pl.BlockSpec((TM, d), lambda i, j: (i, 0)),
        ),
        compiler_params=_params(("arbitrary", "arbitrary")),
        name="in_proj",
    )(x2, g, mods, mods, cos_t, sin_t, w_in_t)


def _pool_kernel(a_ref, ap_ref, an_ref, w_ref, s_ref, o_ref, ext_ref, *, tps, seq):
    tp = a_ref.shape[0]
    ti = pl.program_id(0) % tps
    ext_ref[0:POOL_HALO, :] = jnp.where(ti == 0, 0.0, ap_ref[...])
    ext_ref[POOL_HALO:POOL_HALO + tp, :] = a_ref[...]
    ext_ref[POOL_HALO + tp:2 * POOL_HALO + tp, :] = jnp.where(ti == tps - 1, 0.0, an_ref[...])
    t = ti * tp + lax.broadcasted_iota(jnp.int32, (tp, 1), 0)
    for gi, w in enumerate(POOL_WINDOWS):
        cols = slice(gi * POOL_GROUP, (gi + 1) * POOL_GROUP)
        half = w // 2
        acc = ext_ref[pl.ds(POOL_HALO - half, tp), cols]
        for k in range(-half + 1, half):
            acc = acc + ext_ref[pl.ds(POOL_HALO + k, tp), cols]
        count = (jnp.minimum(t + half, seq) - jnp.maximum(t - half, 0)).astype(F32)
        pooled = acc / count - a_ref[:, cols]
        y = jnp.dot(pooled.astype(BF16), w_ref[0, gi], preferred_element_type=F32) * s_ref[0, :, cols]
        o_ref[:, cols] = y.astype(BF16)


def _pool_branch(a, layer, pool_w, pool_scale, seq):
    rows, w = a.shape
    tp = POOL_TILE
    tps = seq // tp
    hb = tp // POOL_HALO
    nhalo = rows // POOL_HALO
    return pl.pallas_call(
        functools.partial(_pool_kernel, tps=tps, seq=seq),
        out_shape=jax.ShapeDtypeStruct((rows, w), BF16),
        grid=(rows // tp,),
        in_specs=[
            pl.BlockSpec((tp, w), lambda i: (i, 0)),
            pl.BlockSpec((POOL_HALO, w), lambda i: (jnp.maximum(i * hb - 1, 0), 0)),
            pl.BlockSpec((POOL_HALO, w), lambda i: (jnp.minimum((i + 1) * hb, nhalo - 1), 0)),
            pl.BlockSpec((1, len(POOL_WINDOWS), POOL_GROUP, POOL_GROUP), lambda i: (layer, 0, 0, 0)),
            pl.BlockSpec((1, 1, w), lambda i: (layer, 0, 0)),
        ],
        out_specs=pl.BlockSpec((tp, w), lambda i: (i, 0)),
        scratch_shapes=[pltpu.VMEM((tp + 2 * POOL_HALO, w), F32)],
        compiler_params=_params(("arbitrary",)),
        name="pool",
    )(a, a, a, pool_w, pool_scale)


def _f1_kernel(x_ref, m1_ref, tc_ref, ts_ref, yr_ref, yi_ref):
    n1 = x_ref.shape[1]
    y = jnp.dot(m1_ref[...], x_ref[0], preferred_element_type=F32)
    yr = y[:n1]
    yi = y[n1:]
    c = tc_ref[...]
    s = ts_ref[...]
    yr_ref[0] = (yr * c + yi * s).astype(BF16)
    yi_ref[0] = (yi * c - yr * s).astype(BF16)


def _f2_kernel(yr_ref, yi_ref, ma_ref, mb_ref, cc_ref, cs_ref, o_ref, zr_ref, zi_ref, *, scale):
    n2 = F_T2
    w = o_ref.shape[2] // F2_TILE
    for s in range(F2_TILE):
        rows = slice(s * n2, (s + 1) * n2)
        z = (jnp.dot(ma_ref[...], yr_ref[0, rows, :], preferred_element_type=F32)
             + jnp.dot(mb_ref[...], yi_ref[0, rows, :], preferred_element_type=F32))
        zr_ref[rows, :] = z[:n2].astype(BF16)
        zi_ref[rows, :] = z[n2:].astype(BF16)
    o = (jnp.dot(zr_ref[...], cc_ref[...], preferred_element_type=F32)
         + jnp.dot(zi_ref[...], cs_ref[...], preferred_element_type=F32)) * scale
    for s in range(F2_TILE):
        o_ref[0, :, s * w:(s + 1) * w] = o[s * n2:(s + 1) * n2].astype(BF16)


def _dft_cos_sin(n):
    k = np.arange(n)
    ang = 2.0 * np.pi * ((k[:, None] * k[None, :]) % n) / n
    return np.cos(ang), np.sin(ang)


def _fourier_consts(seq):
    n2 = F_T2
    n1 = seq // n2
    c1, s1 = _dft_cos_sin(n1)
    c2, s2 = _dft_cos_sin(n2)
    cg, sg = _dft_cos_sin(FOURIER_GROUP)
    eye = np.eye(BRANCH_WIDTH // FOURIER_GROUP)
    m1 = jnp.asarray(np.concatenate([c1, -s1], axis=0), BF16)
    ma = jnp.asarray(np.concatenate([c2, -s2], axis=0), BF16)
    mb = jnp.asarray(np.concatenate([s2, c2], axis=0), BF16)
    cc = jnp.asarray(np.kron(eye, cg), BF16)
    cs = jnp.asarray(np.kron(eye, sg), BF16)
    prod = (jnp.arange(n1, dtype=jnp.int32)[:, None] * jnp.arange(n2, dtype=jnp.int32)[None, :]) % seq
    ang = prod.astype(F32) * (2.0 * math.pi / seq)
    shape = (n1, n2, BRANCH_WIDTH)
    tc = jnp.broadcast_to(jnp.cos(ang)[:, :, None], shape).reshape(n1, n2 * BRANCH_WIDTH)
    ts = jnp.broadcast_to(jnp.sin(ang)[:, :, None], shape).reshape(n1, n2 * BRANCH_WIDTH)
    return dict(m1=m1, ma=ma, mb=mb, cc=cc, cs=cs, tc=tc, ts=ts)


def _fourier_branch(f, fc, batch, seq):
    w = BRANCH_WIDTH
    n2 = F_T2
    n1 = seq // n2
    blk = F1_TILE * w
    x3 = f.reshape(batch, n1, n2 * w)
    yr, yi = pl.pallas_call(
        _f1_kernel,
        out_shape=(jax.ShapeDtypeStruct((batch, n1, n2 * w), BF16),) * 2,
        grid=(n2 // F1_TILE, batch),
        in_specs=[
            pl.BlockSpec((1, n1, blk), lambda j, b: (b, 0, j)),
            pl.BlockSpec((2 * n1, n1), lambda j, b: (0, 0)),
            pl.BlockSpec((n1, blk), lambda j, b: (0, j)),
            pl.BlockSpec((n1, blk), lambda j, b: (0, j)),
        ],
        out_specs=(pl.BlockSpec((1, n1, blk), lambda j, b: (b, 0, j)),) * 2,
        compiler_params=_params(("arbitrary", "arbitrary")),
        name="fourier1",
    )(x3, fc["m1"], fc["tc"], fc["ts"])
    yr = yr.reshape(batch, n1 * n2, w)
    yi = yi.reshape(batch, n1 * n2, w)
    rows = F2_TILE * n2
    out = pl.pallas_call(
        functools.partial(_f2_kernel, scale=1.0 / math.sqrt(seq * FOURIER_GROUP)),
        out_shape=jax.ShapeDtypeStruct((batch, n2, n1 * w), BF16),
        grid=(batch, n1 // F2_TILE),
        in_specs=[
            pl.BlockSpec((1, rows, w), lambda b, j: (b, j, 0)),
            pl.BlockSpec((1, rows, w), lambda b, j: (b, j, 0)),
            pl.BlockSpec((2 * n2, n2), lambda b, j: (0, 0)),
            pl.BlockSpec((2 * n2, n2), lambda b, j: (0, 0)),
            pl.BlockSpec((w, w), lambda b, j: (0, 0)),
            pl.BlockSpec((w, w), lambda b, j: (0, 0)),
        ],
        out_specs=pl.BlockSpec((1, n2, F2_TILE * w), lambda b, j: (b, 0, j)),
        scratch_shapes=[pltpu.VMEM((rows, w), BF16), pltpu.VMEM((rows, w), BF16)],
        compiler_params=_params(("arbitrary", "arbitrary")),
        name="fourier2",
    )(yr, yi, fc["ma"], fc["mb"], fc["cc"], fc["cs"])
    return out.reshape(batch * seq, w)


def _log_sigmoid(x):
    return jnp.minimum(x, 0.0) - jnp.log1p(jnp.exp(-jnp.abs(x)))


def _ret_kernel(q_ref, k_ref, v_ref, g_ref, df_ref, db_ref, o_ref, kv_ref, r_ref, *, seq):
    c = RET_CHUNK
    d = RET_HEAD_DIM
    nc = seq // c
    lgf = _log_sigmoid(df_ref[0, 0])
    lgb = _log_sigmoid(db_ref[0, 0])
    ii = lax.broadcasted_iota(jnp.int32, (c, 1), 0).astype(F32)
    kdf = jnp.exp(lgf * (c - 1.0 - ii))
    kdb = jnp.exp(lgb * ii)
    qdf = jnp.exp(lgf * (ii + 1.0))
    qdb = jnp.exp(lgb * (c - ii))
    diff = (lax.broadcasted_iota(jnp.int32, (c, c), 0)
            - lax.broadcasted_iota(jnp.int32, (c, c), 1)).astype(F32)
    lgf_c = jnp.concatenate([lgf] * (c // d), axis=1)
    lgb_c = jnp.concatenate([lgb] * (c // d), axis=1)
    dmat = jnp.where(diff >= 0.0, jnp.exp(lgf_c * jnp.maximum(diff, 0.0)),
                     jnp.exp(lgb_c * jnp.maximum(-diff, 0.0)))
    chunk_f = jnp.exp(lgf * float(c))
    chunk_b = jnp.exp(lgb * float(c))

    def kv_body(n, carry):
        rows = pl.ds(pl.multiple_of(n * c, c), c)
        kc = k_ref[0, rows, :].astype(F32)
        kd = jnp.concatenate([kc * kdf, kc * kdb], axis=1).astype(BF16)
        kv_ref[n] = lax.dot_general(kd, v_ref[0, rows, :], (((0,), (0,)), ((), ())),
                                    preferred_element_type=F32)
        return carry

    lax.fori_loop(0, nc, kv_body, 0, unroll=2)

    def fwd_body(n, state):
        r_ref[n, 0:d, :] = state.astype(BF16)
        return chunk_f * state + kv_ref[n, 0:d, :]

    lax.fori_loop(0, nc, fwd_body, jnp.zeros((d, d), F32))

    def bwd_body(m, state):
        n = nc - 1 - m
        r_ref[n, d:2 * d, :] = state.astype(BF16)
        return chunk_b * state + kv_ref[n, d:2 * d, :]

    lax.fori_loop(0, nc, bwd_body, jnp.zeros((d, d), F32))

    def out_body(t, carry):
        staged = []
        for n in (2 * t, 2 * t + 1):
            rows = pl.ds(pl.multiple_of(n * c, c), c)
            qb = q_ref[0, rows, :]
            s = lax.dot_general(qb, k_ref[0, rows, :], (((1,), (1,)), ((), ())),
                                preferred_element_type=F32)
            qc = qb.astype(F32)
            qd = jnp.concatenate([qc * qdf, qc * qdb], axis=1).astype(BF16)
            staged.append((rows, s, jnp.dot(qd, r_ref[n], preferred_element_type=F32)))
        for rows, s, cross in staged:
            o = jnp.dot((s * dmat).astype(BF16), v_ref[0, rows, :], preferred_element_type=F32) + cross
            mu = jnp.mean(o, axis=-1, keepdims=True)
            var = jnp.mean(jnp.square(o - mu), axis=-1, keepdims=True)
            on = (o - mu) * lax.rsqrt(var + EPS)
            g = g_ref[0, rows, :].astype(F32)
            o_ref[0, rows, :] = (on * (g * jax.nn.sigmoid(g))).astype(BF16)
        return carry

    lax.fori_loop(0, nc // 2, out_body, 0)


def _ret_branch(ret, layer, decay_f, decay_b, batch, seq):
    d = RET_HEAD_DIM
    nc = seq // RET_CHUNK
    r3 = ret.reshape(batch, seq, 4 * BRANCH_WIDTH)

    def sec(k):
        return pl.BlockSpec((1, seq, d), lambda b, h: (b, 0, k * RET_HEADS + h))

    decay_spec = pl.BlockSpec((1, 1, 1, d), lambda b, h: (layer, h, 0, 0))
    out = pl.pallas_call(
        functools.partial(_ret_kernel, seq=seq),
        out_shape=jax.ShapeDtypeStruct((batch, seq, BRANCH_WIDTH), BF16),
        grid=(batch, RET_HEADS),
        in_specs=[sec(0), sec(1), sec(2), sec(3), decay_spec, decay_spec],
        out_specs=pl.BlockSpec((1, seq, d), lambda b, h: (b, 0, h)),
        scratch_shapes=[pltpu.VMEM((nc, 2 * d, d), F32), pltpu.VMEM((nc, 2 * d, d), BF16)],
        compiler_params=_params(("arbitrary", "arbitrary")),
        name="retention",
    )(r3, r3, r3, r3, decay_f, decay_b)
    return out.reshape(batch * seq, BRANCH_WIDTH)


def _na_kernel(q_ref, k0_ref, k1_ref, k2_ref, v0_ref, v1_ref, v2_ref, bias_ref, o_ref,
               kw_ref, vw_ref, *, grid_rows):
    rr = NA_ROWS_PER_STEP
    gw = GRID_W
    blk = rr * gw
    kr = NA_ROWS_MAX
    m = pl.program_id(1)
    for idx, (kref, vref) in enumerate(((k0_ref, v0_ref), (k1_ref, v1_ref), (k2_ref, v2_ref))):
        kw_ref[idx * blk:(idx + 1) * blk, :] = kref[0]
        vw_ref[idx * blk:(idx + 1) * blk, :] = vref[0]
    lane = lax.broadcasted_iota(jnp.int32, (gw, 2 * NA_HEAD_DIM), 1)
    low = lane < NA_HEAD_DIM

    pair_cols = [slice(jp * 2 * NA_HEAD_DIM, (jp + 1) * 2 * NA_HEAD_DIM) for jp in range(NA_HEADS // 2)]

    def window(r):
        r_glob = m * rr + r
        rs = jnp.clip(r_glob - kr // 2, 0, grid_rows - kr)
        krows = pl.ds(pl.multiple_of((rs - (m - 1) * rr) * gw, gw), kr * gw)
        qrows = pl.ds(pl.multiple_of(r * gw, gw), gw)
        return qrows, krows, r_glob - rs

    def scores(r):
        qrows, krows, didx = window(r)
        out = []
        for jp, cols in enumerate(pair_cols):
            q2 = q_ref[0, qrows, cols]
            k2 = kw_ref[krows, cols]
            for e in range(2):
                qm = jnp.where(low if e == 0 else jnp.logical_not(low), q2, jnp.zeros_like(q2))
                s = lax.dot_general(qm, k2, (((1,), (1,)), ((), ())), preferred_element_type=F32)
                out.append(s + bias_ref[0, didx, 2 * jp + e])
        return out

    def softmax(ss):
        out = []
        for s in ss:
            p = jnp.exp(s - jnp.max(s, axis=-1, keepdims=True))
            out.append((p.astype(BF16), jnp.sum(p, axis=-1, keepdims=True)))
        return out

    def values(r, ps):
        qrows, krows, _ = window(r)
        for jp, cols in enumerate(pair_cols):
            v2 = vw_ref[krows, cols]
            outs = [jnp.dot(p, v2, preferred_element_type=F32) / l for p, l in ps[2 * jp:2 * jp + 2]]
            o_ref[0, qrows, cols] = jnp.where(low, outs[0], outs[1]).astype(BF16)

    def group_body(t, carry):
        rows = [NA_ROW_GROUP * t + u for u in range(NA_ROW_GROUP)]
        staged = [scores(r) for r in rows]
        for r, ss in zip(rows, staged):
            values(r, softmax(ss))
        return carry

    lax.fori_loop(0, rr // NA_ROW_GROUP, group_body, 0)


def _na_bias(rpb):
    rpb = rpb.astype(F32)
    per_q = []
    for qc in range(GRID_W):
        cs = min(max(qc - NA_COLS // 2, 0), GRID_W - NA_COLS)
        lo = (NA_COLS - 1) - qc + cs
        seg = rpb[..., lo:lo + NA_COLS]
        per_q.append(jnp.pad(seg, ((0, 0),) * 3 + ((cs, GRID_W - NA_COLS - cs),), constant_values=NEG_INF))
    toep = jnp.stack(per_q, axis=3)
    per_dl = []
    for dl in range(NA_ROWS_MAX):
        lo = NA_ROWS_MAX - 1 - dl
        win = toep[:, :, lo:lo + NA_ROWS_MAX]
        per_dl.append(jnp.transpose(win, (0, 1, 3, 2, 4)).reshape(
            rpb.shape[0], NA_HEADS, GRID_W, NA_ROWS_MAX * GRID_W))
    return jnp.stack(per_dl, axis=1)


def _na_branch(na, layer, bias, batch, seq):
    w = BRANCH_WIDTH
    grid_rows = seq // GRID_W
    rr = NA_ROWS_PER_STEP
    blk = rr * GRID_W
    nblk = grid_rows // rr
    n3 = na.reshape(batch, seq, 3 * w)

    def kv_spec(col, shift):
        return pl.BlockSpec((1, blk, w), lambda b, m: (b, jnp.clip(m + shift, 0, nblk - 1), col))

    out = pl.pallas_call(
        functools.partial(_na_kernel, grid_rows=grid_rows),
        out_shape=jax.ShapeDtypeStruct((batch, seq, w), BF16),
        grid=(batch, nblk),
        in_specs=[pl.BlockSpec((1, blk, w), lambda b, m: (b, m, 0)),
                  kv_spec(1, -1), kv_spec(1, 0), kv_spec(1, 1),
                  kv_spec(2, -1), kv_spec(2, 0), kv_spec(2, 1),
                  pl.BlockSpec((1,) + bias.shape[1:], lambda b, m: (layer, 0, 0, 0, 0))],
        out_specs=pl.BlockSpec((1, blk, w), lambda b, m: (b, m, 0)),
        scratch_shapes=[pltpu.VMEM((3 * blk, w), BF16), pltpu.VMEM((3 * blk, w), BF16)],
        compiler_params=_params(("arbitrary", "arbitrary")),
        name="nbr_attn",
    )(n3, n3, n3, n3, n3, n3, n3, bias)
    return out.reshape(batch * seq, w)


def _merge_kernel(h_ref, gate_ref, xs_ref,
                  br0, br1, br2, br3, wg0, wg1, wg2, wg3, bg0, bg1, bg2, bg3,
                  wb0, wb1, wb2, wb3, wo_ref, o_ref, m_ref, *, nj):
    j = pl.program_id(1)

    @pl.when(j < nj)
    def _():
        acc = None
        for br, wg, bg, wb in ((br0, wg0, bg0, wb0), (br1, wg1, bg1, wb1),
                               (br2, wg2, bg2, wb2), (br3, wg3, bg3, wb3)):
            gate = jax.nn.sigmoid(jnp.dot(h_ref[...], wg[0], preferred_element_type=F32) + bg[0])
            term = gate * jnp.dot(br[...], wb[0, 0], preferred_element_type=F32)
            acc = term if acc is None else acc + term
        m_ref[j] = acc.astype(BF16)

    @pl.when(j >= nj)
    def _():
        tn = wo_ref.shape[2]
        out = jnp.dot(m_ref[0], wo_ref[0, 0:tn, :], preferred_element_type=F32)
        for kk in range(1, nj):
            out = out + jnp.dot(m_ref[kk], wo_ref[0, kk * tn:(kk + 1) * tn, :], preferred_element_type=F32)
        o_ref[...] = xs_ref[...] + gate_ref[0] * out


def _merge(x2, h, layer, mods, branches, w_gate_t, b_gate, w_branch_t, w_out_t, seq):
    rows, d = x2.shape
    tn = TN_MERGE
    nj = d // tn
    tps = seq // TM

    def first(j):
        return jnp.minimum(j, nj - 1)

    def second(j):
        return jnp.maximum(j - nj, 0)

    def gate_w_spec(b):
        return pl.BlockSpec((1, d, tn), lambda i, j: (layer, 0, b * nj + first(j)))

    def gate_b_spec(b):
        return pl.BlockSpec((1, 1, tn), lambda i, j: (layer, 0, b * nj + first(j)))

    def branch_w_spec(b):
        return pl.BlockSpec((1, 1, BRANCH_WIDTH, tn), lambda i, j: (layer, b, 0, first(j)))

    br_spec = pl.BlockSpec((TM, BRANCH_WIDTH), lambda i, j: (i, 0))
    in_specs = [
        pl.BlockSpec((TM, d), lambda i, j: (i, 0)),
        pl.BlockSpec((1, 1, tn), lambda i, j: ((i // tps) * 6 + 2, 0, second(j))),
        pl.BlockSpec((TM, tn), lambda i, j: (i, second(j))),
        br_spec, br_spec, br_spec, br_spec,
        gate_w_spec(0), gate_w_spec(1), gate_w_spec(2), gate_w_spec(3),
        gate_b_spec(0), gate_b_spec(1), gate_b_spec(2), gate_b_spec(3),
        branch_w_spec(0), branch_w_spec(1), branch_w_spec(2), branch_w_spec(3),
        pl.BlockSpec((1, d, tn), lambda i, j: (layer, 0, second(j))),
    ]
    return pl.pallas_call(
        functools.partial(_merge_kernel, nj=nj),
        out_shape=jax.ShapeDtypeStruct((rows, d), F32),
        grid=(rows // TM, 2 * nj),
        in_specs=in_specs,
        out_specs=pl.BlockSpec((TM, tn), lambda i, j: (i, second(j))),
        scratch_shapes=[pltpu.VMEM((nj, TM, tn), BF16)],
        compiler_params=_params(("arbitrary", "arbitrary")),
        name="merge",
    )(h, mods, x2, *branches,
      w_gate_t, w_gate_t, w_gate_t, w_gate_t, b_gate, b_gate, b_gate, b_gate,
      w_branch_t, w_branch_t, w_branch_t, w_branch_t, w_out_t)


def _ffn_kernel(x_ref, xp_ref, xn_ref, g_ref, sc_ref, sh_ref, gate_ref, xs_ref,
                wa_ref, wb_ref, cwa_ref, cwb_ref, cba_ref, cbb_ref, wd_ref,
                o_ref, h_ref, ua0_ref, ub0_ref, ua1_ref, ub1_ref, act_ref, *, tps, nf):
    i = pl.program_id(0)
    j = pl.program_id(1)
    tm = x_ref.shape[0]
    tf = ua0_ref.shape[1]
    ti = i % tps

    ch = FFN_CHUNK
    ext = ch + 2 * 8

    bufs = ((ua0_ref, ub0_ref), (ua1_ref, ub1_ref))

    def up(slot):
        h = h_ref[...]
        bufs[slot][0][...] = jnp.dot(h, wa_ref[0], preferred_element_type=F32)
        bufs[slot][1][...] = jnp.dot(h, wb_ref[0], preferred_element_type=F32)

    def conv(u_ref, cw_ref, cb_ref, r0):
        u = u_ref[r0 + HALO - 8:r0 + HALO - 8 + ext, :]
        prev = pltpu.roll(u, 1, 0)[8:8 + ch]
        nxt = pltpu.roll(u, ext - 1, 0)[8:8 + ch]
        return (prev * cw_ref[0, 0:1, :] + u[8:8 + ch] * cw_ref[0, 1:2, :]
                + nxt * cw_ref[0, 2:3, :] + cb_ref[0])

    def activate(slot, tile):
        for r0 in range(0, tm, ch):
            a = conv(bufs[slot][0], cwa_ref, cba_ref, r0)
            b = conv(bufs[slot][1], cwb_ref, cbb_ref, r0)
            act_ref[tile, r0:r0 + ch, :] = (jax.nn.gelu(a, approximate=True) * b).astype(BF16)

    def down():
        out = jnp.dot(act_ref[0], wd_ref[0, 0:tf, :], preferred_element_type=F32)
        for kk in range(1, nf):
            out = out + jnp.dot(act_ref[kk], wd_ref[0, kk * tf:(kk + 1) * tf, :],
                                preferred_element_type=F32)
        o_ref[...] = xs_ref[...] + gate_ref[0] * out

    @pl.when(j == 0)
    def _():
        g = g_ref[0]
        sc = sc_ref[0]
        sh = sh_ref[0]
        _norm_rows(xp_ref, h_ref, 0, g, sc, sh, keep=ti != 0)
        _norm_rows(x_ref, h_ref, HALO, g, sc, sh)
        _norm_rows(xn_ref, h_ref, HALO + tm, g, sc, sh, keep=ti != tps - 1)
        up(0)

    for parity in range(2):
        @pl.when((j >= 1) & (j < nf) & (j % 2 == parity))
        def _():
            activate(1 - parity, j - 1)
            up(parity)

    @pl.when(j == nf)
    def _():
        activate((nf - 1) % 2, nf - 1)
        down()

    @pl.when(j > nf)
    def _():
        down()


def _ffn(x2, layer, g, mods, w_up_t, conv_w, conv_b, w_down_t, seq):
    rows, d = x2.shape
    tm = TM_FFN
    tps = seq // tm
    nf = D_FF // TF
    tn = TN_DOWN
    nd = d // tn
    hb = tm // HALO
    nhalo = rows // HALO

    def first(j):
        return jnp.minimum(j, nf - 1)

    def second(j):
        return jnp.maximum(j - nf, 0)

    def lagged(j):
        return jnp.clip(j - 1, 0, nf - 1)

    return pl.pallas_call(
        functools.partial(_ffn_kernel, tps=tps, nf=nf),
        out_shape=jax.ShapeDtypeStruct((rows, d), F32),
        grid=(rows // tm, nf + nd),
        in_specs=[
            pl.BlockSpec((tm, d), lambda i, j: (i, 0)),
            pl.BlockSpec((HALO, d), lambda i, j: (jnp.maximum(i * hb - 1, 0), 0)),
            pl.BlockSpec((HALO, d), lambda i, j: (jnp.minimum((i + 1) * hb, nhalo - 1), 0)),
            pl.BlockSpec((1, 1, d), lambda i, j: (layer, 0, 0)),
            pl.BlockSpec((1, 1, d), lambda i, j: ((i // tps) * 6 + 4, 0, 0)),
            pl.BlockSpec((1, 1, d), lambda i, j: ((i // tps) * 6 + 3, 0, 0)),
            pl.BlockSpec((1, 1, tn), lambda i, j: ((i // tps) * 6 + 5, 0, second(j))),
            pl.BlockSpec((tm, tn), lambda i, j: (i, second(j))),
            pl.BlockSpec((1, d, TF), lambda i, j: (layer, 0, first(j))),
            pl.BlockSpec((1, d, TF), lambda i, j: (layer, 0, nf + first(j))),
            pl.BlockSpec((1, 3, TF), lambda i, j: (layer, 0, lagged(j))),
            pl.BlockSpec((1, 3, TF), lambda i, j: (layer, 0, nf + lagged(j))),
            pl.BlockSpec((1, 1, TF), lambda i, j: (layer, 0, lagged(j))),
            pl.BlockSpec((1, 1, TF), lambda i, j: (layer, 0, nf + lagged(j))),
            pl.BlockSpec((1, D_FF, tn), lambda i, j: (layer, 0, second(j))),
        ],
        out_specs=pl.BlockSpec((tm, tn), lambda i, j: (i, second(j))),
        scratch_shapes=[pltpu.VMEM((tm + 2 * HALO, d), BF16),
                        pltpu.VMEM((tm + 2 * HALO, TF), F32),
                        pltpu.VMEM((tm + 2 * HALO, TF), F32),
                        pltpu.VMEM((tm + 2 * HALO, TF), F32),
                        pltpu.VMEM((tm + 2 * HALO, TF), F32),
                        pltpu.VMEM((nf, tm, TF), BF16)],
        compiler_params=_params(("arbitrary", "arbitrary")),
        name="conv_ffn",
    )(x2, x2, x2, g, mods, mods, mods, x2, w_up_t, w_up_t, conv_w, conv_w, conv_b, conv_b, w_down_t)


def _final_kernel(x_ref, g_ref, o_ref):
    x = x_ref[...]
    ms = jnp.mean(x * x, axis=-1, keepdims=True)
    o_ref[...] = x * lax.rsqrt(ms + EPS) * g_ref[...]


def _final_norm(x2, g_final):
    rows, d = x2.shape
    tr = 256
    return pl.pallas_call(
        _final_kernel,
        out_shape=jax.ShapeDtypeStruct((rows, d), F32),
        grid=(rows // tr,),
        in_specs=[pl.BlockSpec((tr, d), lambda i: (i, 0)), pl.BlockSpec((1, d), lambda i: (0, 0))],
        out_specs=pl.BlockSpec((tr, d), lambda i: (i, 0)),
        compiler_params=_params(("arbitrary",)),
        name="final_norm",
    )(x2, g_final)


def _rope_tables(seq):
    half = RET_HEAD_DIM // 2
    inv = ROPE_BASE ** (-jnp.arange(half, dtype=F32) / half)
    ang = jnp.arange(seq, dtype=F32)[:, None] * inv[None, :]
    cos = jnp.cos(ang)
    sin = jnp.sin(ang)
    return jnp.concatenate([cos, cos], axis=1), jnp.concatenate([-sin, sin], axis=1)


def _trunk(x, mods, wts):
    batch, seq, d = x.shape
    x2 = x.reshape(batch * seq, d)
    cos_t, sin_t = _rope_tables(seq)
    fc = _fourier_consts(seq)
    for l in range(DEPTH):
        m = mods[l]
        pool_in, four_in, ret_in, na_in, h = _in_proj(x2, l, wts["g_mix"], m, cos_t, sin_t, wts["w_in"], seq)
        branches = (
            _pool_branch(pool_in, l, wts["pool_w"], wts["pool_scale"], seq),
            _fourier_branch(four_in, fc, batch, seq),
            _ret_branch(ret_in, l, wts["ret_decay_f"], wts["ret_decay_b"], batch, seq),
            _na_branch(na_in, l, wts["na_bias"], batch, seq),
        )
        x2 = _merge(x2, h, l, m, branches, wts["w_gate"], wts["b_gate"], wts["w_branch"], wts["w_out"], seq)
        x2 = _ffn(x2, l, wts["g_ffn"], m, wts["w_up"], wts["conv_w"], wts["conv_b"], wts["w_down"], seq)
    return _final_norm(x2, wts["g_final"]).reshape(batch, seq, d)


def kernel(x_prompt, x_sample, c_prompt, c_sample, w_ada, b_ada, g_mix, w_in, pool_w, pool_scale,
           ret_decay_f, ret_decay_b, na_rpb, w_branch, w_gate, b_gate, w_out, g_ffn, w_up, conv_w,
           conv_b, w_down, g_final):
    d = D_MODEL
    nb_p = c_prompt.shape[0]
    nb_s = c_sample.shape[0]
    c8 = jnp.concatenate([c_prompt, c_sample, jnp.zeros((8 - nb_p - nb_s, d), F32)], axis=0)
    mod_all = _ada(c8, w_ada, b_ada)
    mods_p = mod_all[:, :nb_p].reshape(DEPTH, nb_p * 6, 1, d)
    mods_s = mod_all[:, nb_p:nb_p + nb_s].reshape(DEPTH, nb_s * 6, 1, d)
    decay_shape = (DEPTH, RET_HEADS, 1, RET_HEAD_DIM)
    wts = dict(
        g_mix=g_mix.reshape(DEPTH, 1, d),
        w_in=w_in.astype(BF16),
        pool_w=pool_w.astype(BF16),
        pool_scale=pool_scale.reshape(DEPTH, 1, BRANCH_WIDTH),
        ret_decay_f=jnp.broadcast_to(ret_decay_f.astype(F32)[:, :, None, None], decay_shape),
        ret_decay_b=jnp.broadcast_to(ret_decay_b.astype(F32)[:, :, None, None], decay_shape),
        na_bias=_na_bias(na_rpb),
        w_branch=w_branch.astype(BF16),
        w_gate=w_gate.astype(BF16),
        b_gate=b_gate.reshape(DEPTH, 1, N_BRANCH * d),
        w_out=w_out.astype(BF16),
        g_ffn=g_ffn.reshape(DEPTH, 1, d),
        w_up=w_up.astype(BF16),
        conv_w=conv_w,
        conv_b=conv_b.reshape(DEPTH, 1, 2 * D_FF),
        w_down=w_down.astype(BF16),
        g_final=g_final.reshape(1, d),
    )
    y_prompt = _trunk(x_prompt, mods_p, wts)
    y_sample = _trunk(x_sample, mods_s, wts)
    return (y_prompt, y_sample)
```

```python
import functools
import math

import numpy as np
import jax
import jax.numpy as jnp
from jax import lax
from jax.experimental import pallas as pl
from jax.experimental.pallas import tpu as pltpu

F32 = jnp.float32
BF16 = jnp.bfloat16

D_MODEL = 2048
DEPTH = 4
GRID_W = 64
BRANCH_WIDTH = 512
N_BRANCH = 4
POOL_WINDOWS = (2, 4, 8, 16)
POOL_GROUP = 128
FOURIER_GROUP = 128
RET_HEADS = 4
RET_HEAD_DIM = 128
ROPE_BASE = 10000.0
NA_HEADS = 8
NA_HEAD_DIM = 64
NA_ROWS_MAX = 8
NA_COLS = 16
D_FF = 5632
EPS = 1e-6
NEG_INF = -1e30
IN_WIDTH = 9 * BRANCH_WIDTH

VMEM_LIMIT_BYTES = 56 * 1024 * 1024
BF16_SUBLANES = 16

TM = 1024
TN_IN = 512
TN_MERGE = 256
TM_FFN = 1024
TF = 256
TN_DOWN = 256
HALO = BF16_SUBLANES
NORM_CHUNK = 32
NORM_UNROLL = 4
FFN_CHUNK = 64
POOL_TILE = 512
POOL_HALO = 8
F_T2 = 128
F1_TILE = 8
F2_TILE = 8
RET_CHUNK = 256
NA_ROWS_PER_STEP = 8
NA_ROW_GROUP = 4


def _params(sem):
    return pltpu.CompilerParams(dimension_semantics=sem, vmem_limit_bytes=VMEM_LIMIT_BYTES)


def _norm_rows(x_ref, dst_ref, dst_off, g, scale, shift, keep=None):
    rows = x_ref.shape[0]
    ch = min(rows, NORM_CHUNK)
    gain = g * (1.0 + scale)

    def body(c, carry):
        r0 = pl.multiple_of(c * ch, ch)
        x = x_ref[pl.ds(r0, ch), :]
        ms = jnp.mean(x * x, axis=-1, keepdims=True)
        h = (x * lax.rsqrt(ms + EPS)) * gain + shift
        if keep is not None:
            h = jnp.where(keep, h, 0.0)
        dst_ref[pl.ds(dst_off + r0, ch), :] = h.astype(BF16)
        return carry

    lax.fori_loop(0, rows // ch, body, 0, unroll=min(NORM_UNROLL, rows // ch))


def _ada_kernel(c_ref, w_ref, b_ref, o_ref):
    c = c_ref[...]
    s = (c * jax.nn.sigmoid(c)).astype(BF16)
    o_ref[0] = jnp.dot(s, w_ref[0].astype(BF16), preferred_element_type=F32) + b_ref[0]


def _ada(c8, w_ada, b_ada):
    depth, d, n = w_ada.shape
    tn = 1024
    return pl.pallas_call(
        _ada_kernel,
        out_shape=jax.ShapeDtypeStruct((depth, 8, n), F32),
        grid=(depth, n // tn),
        in_specs=[
            pl.BlockSpec((8, d), lambda l, j: (0, 0)),
            pl.BlockSpec((1, d, tn), lambda l, j: (l, 0, j)),
            pl.BlockSpec((1, 1, tn), lambda l, j: (l, 0, j)),
        ],
        out_specs=pl.BlockSpec((1, 8, tn), lambda l, j: (l, 0, j)),
        compiler_params=_params(("arbitrary", "arbitrary")),
        name="ada",
    )(c8, w_ada, b_ada.reshape(depth, 1, n))


def _rope_tile(acc, cos, sin, scale):
    parts = []
    for hh in range(acc.shape[1] // RET_HEAD_DIM):
        ch = acc[:, hh * RET_HEAD_DIM:(hh + 1) * RET_HEAD_DIM]
        parts.append((ch * cos + pltpu.roll(ch, RET_HEAD_DIM // 2, 1) * sin) * scale)
    return jnp.concatenate(parts, axis=1)


def _in_kernel(x_ref, g_ref, sc_ref, sh_ref, cos_ref, sin_ref, w_ref,
               pool_ref, four_ref, ret_ref, na_ref, h_ref):
    j = pl.program_id(1)

    @pl.when(j == 0)
    def _():
        _norm_rows(x_ref, h_ref, 0, g_ref[0], sc_ref[0], sh_ref[0])

    def proj():
        return jnp.dot(h_ref[...], w_ref[0], preferred_element_type=F32)

    @pl.when(j == 0)
    def _():
        pool_ref[...] = proj()

    @pl.when(j == 1)
    def _():
        four_ref[...] = proj()

    @pl.when(j == 2)
    def _():
        ret_ref[...] = _rope_tile(proj(), cos_ref[...], sin_ref[...], 1.0).astype(BF16)

    @pl.when(j == 3)
    def _():
        ret_ref[...] = _rope_tile(proj(), cos_ref[...], sin_ref[...], RET_HEAD_DIM ** -0.5).astype(BF16)

    @pl.when((j == 4) | (j == 5))
    def _():
        ret_ref[...] = proj().astype(BF16)

    @pl.when(j == 6)
    def _():
        na_ref[...] = (proj() * (NA_HEAD_DIM ** -0.5)).astype(BF16)

    @pl.when(j >= 7)
    def _():
        na_ref[...] = proj().astype(BF16)


def _in_proj(x2, layer, g, mods, cos_t, sin_t, w_in_t, seq):
    rows, d = x2.shape
    tps = seq // TM
    nj = IN_WIDTH // TN_IN

    def mod_spec(k):
        return pl.BlockSpec((1, 1, d), lambda i, j: ((i // tps) * 6 + k, 0, 0))

    return pl.pallas_call(
        _in_kernel,
        out_shape=(
            jax.ShapeDtypeStruct((rows, BRANCH_WIDTH), F32),
            jax.ShapeDtypeStruct((rows, BRANCH_WIDTH), F32),
            jax.ShapeDtypeStruct((rows, 4 * BRANCH_WIDTH), BF16),
            jax.ShapeDtypeStruct((rows, 3 * BRANCH_WIDTH), BF16),
            jax.ShapeDtypeStruct((rows, d), BF16),
        ),
        grid=(rows // TM, nj),
        in_specs=[
            pl.BlockSpec((TM, d), lambda i, j: (i, 0)),
            pl.BlockSpec((1, 1, d), lambda i, j: (layer, 0, 0)),
            mod_spec(1),
            mod_spec(0),
            pl.BlockSpec((TM, RET_HEAD_DIM), lambda i, j: (i % tps, 0)),
            pl.BlockSpec((TM, RET_HEAD_DIM), lambda i, j: (i % tps, 0)),
            pl.BlockSpec((1, d, TN_IN), lambda i, j: (layer, 0, j)),
        ],
        out_specs=(
            pl.BlockSpec((TM, TN_IN), lambda i, j: (i, 0)),
            pl.BlockSpec((TM, TN_IN), lambda i, j: (i, 0)),
            pl.BlockSpec((TM, TN_IN), lambda i, j: (i, jnp.clip(j - 2, 0, 3))),
            pl.BlockSpec((TM, TN_IN), lambda i, j: (i, jnp.clip(j - 6, 0, 2))),
            pl.BlockSpec((TM, d), lambda i, j: (i, 0)),
        ),
        compiler_params=_params(("arbitrary", "arbitrary")),
        name="in_proj",
    )(x2, g, mods, mods, cos_t, sin_t, w_in_t)


def _pool_kernel(a_ref, ap_ref, an_ref, w_ref, s_ref, o_ref, ext_ref, *, tps, seq):
    tp = a_ref.shape[0]
    ti = pl.program_id(0) % tps
    ext_ref[0:POOL_HALO, :] = jnp.where(ti == 0, 0.0, ap_ref[...])
    ext_ref[POOL_HALO:POOL_HALO + tp, :] = a_ref[...]
    ext_ref[POOL_HALO + tp:2 * POOL_HALO + tp, :] = jnp.where(ti == tps - 1, 0.0, an_ref[...])
    t = ti * tp + lax.broadcasted_iota(jnp.int32, (tp, 1), 0)
    for gi, w in enumerate(POOL_WINDOWS):
        cols = slice(gi * POOL_GROUP, (gi + 1) * POOL_GROUP)
        half = w // 2
        acc = ext_ref[pl.ds(POOL_HALO - half, tp), cols]
        for k in range(-half + 1, half):
            acc = acc + ext_ref[pl.ds(POOL_HALO + k, tp), cols]
        count = (jnp.minimum(t + half, seq) - jnp.maximum(t - half, 0)).astype(F32)
        pooled = acc / count - a_ref[:, cols]
        y = jnp.dot(pooled.astype(BF16), w_ref[0, gi], preferred_element_type=F32) * s_ref[0, :, cols]
        o_ref[:, cols] = y.astype(BF16)


def _pool_branch(a, layer, pool_w, pool_scale, seq):
    rows, w = a.shape
    tp = POOL_TILE
    tps = seq // tp
    hb = tp // POOL_HALO
    nhalo = rows // POOL_HALO
    return pl.pallas_call(
        functools.partial(_pool_kernel, tps=tps, seq=seq),
        out_shape=jax.ShapeDtypeStruct((rows, w), BF16),
        grid=(rows // tp,),
        in_specs=[
            pl.BlockSpec((tp, w), lambda i: (i, 0)),
            pl.BlockSpec((POOL_HALO, w), lambda i: (jnp.maximum(i * hb - 1, 0), 0)),
            pl.BlockSpec((POOL_HALO, w), lambda i: (jnp.minimum((i + 1) * hb, nhalo - 1), 0)),
            pl.BlockSpec((1, len(POOL_WINDOWS), POOL_GROUP, POOL_GROUP), lambda i: (layer, 0, 0, 0)),
            pl.BlockSpec((1, 1, w), lambda i: (layer, 0, 0)),
        ],
        out_specs=pl.BlockSpec((tp, w), lambda i: (i, 0)),
        scratch_shapes=[pltpu.VMEM((tp + 2 * POOL_HALO, w), F32)],
        compiler_params=_params(("arbitrary",)),
        name="pool",
    )(a, a, a, pool_w, pool_scale)


def _f1_kernel(x_ref, k1_ref, tc_ref, ts_ref, yr_ref, yi_ref):
    _, n1, t, w = x_ref.shape
    x = x_ref[0].reshape(n1 * t, w).astype(BF16)
    y = jnp.dot(k1_ref[...], x, preferred_element_type=F32)
    yr = y[:n1 * t]
    yi = y[n1 * t:]
    c = tc_ref[...].reshape(n1 * t, w)
    s = ts_ref[...].reshape(n1 * t, w)
    yr_ref[0] = (yr * c + yi * s).reshape(n1, t, w)
    yi_ref[0] = (yi * c - yr * s).reshape(n1, t, w)


def _f2_kernel(yr_ref, yi_ref, ma_ref, mb_ref, cc_ref, cs_ref, o_ref, zr_ref, zi_ref, *, scale):
    n2 = F_T2
    for s in range(F2_TILE):
        rows = slice(s * n2, (s + 1) * n2)
        z = (jnp.dot(ma_ref[...], yr_ref[0, rows, :].astype(BF16), preferred_element_type=F32)
             + jnp.dot(mb_ref[...], yi_ref[0, rows, :].astype(BF16), preferred_element_type=F32))
        zr_ref[rows, :] = z[:n2].astype(BF16)
        zi_ref[rows, :] = z[n2:].astype(BF16)
    o = (jnp.dot(zr_ref[...], cc_ref[...], preferred_element_type=F32)
         + jnp.dot(zi_ref[...], cs_ref[...], preferred_element_type=F32)) * scale
    for s in range(F2_TILE):
        o_ref[0, :, s, :] = o[s * n2:(s + 1) * n2]


def _dft_cos_sin(n):
    k = np.arange(n)
    ang = 2.0 * np.pi * ((k[:, None] * k[None, :]) % n) / n
    return np.cos(ang), np.sin(ang)


def _fourier_consts(seq):
    n2 = F_T2
    n1 = seq // n2
    c1, s1 = _dft_cos_sin(n1)
    c2, s2 = _dft_cos_sin(n2)
    cg, sg = _dft_cos_sin(FOURIER_GROUP)
    eye = np.eye(BRANCH_WIDTH // FOURIER_GROUP)
    k1 = jnp.asarray(np.kron(np.concatenate([c1, -s1], axis=0), np.eye(F1_TILE)), BF16)
    ma = jnp.asarray(np.concatenate([c2, -s2], axis=0), BF16)
    mb = jnp.asarray(np.concatenate([s2, c2], axis=0), BF16)
    cc = jnp.asarray(np.kron(eye, cg), BF16)
    cs = jnp.asarray(np.kron(eye, sg), BF16)
    prod = (jnp.arange(n1, dtype=jnp.int32)[:, None] * jnp.arange(n2, dtype=jnp.int32)[None, :]) % seq
    ang = prod.astype(F32) * (2.0 * math.pi / seq)
    shape = (n1, n2, BRANCH_WIDTH)
    tc = jnp.broadcast_to(jnp.cos(ang)[:, :, None], shape)
    ts = jnp.broadcast_to(jnp.sin(ang)[:, :, None], shape)
    return dict(k1=k1, ma=ma, mb=mb, cc=cc, cs=cs, tc=tc, ts=ts)


def _fourier_branch(f, fc, batch, seq):
    w = BRANCH_WIDTH
    n2 = F_T2
    n1 = seq // n2
    x4 = f.reshape(batch, n1, n2, w)
    y_spec = pl.BlockSpec((1, n1, F1_TILE, w), lambda j, b: (b, 0, j, 0))
    tw_spec = pl.BlockSpec((n1, F1_TILE, w), lambda j, b: (0, j, 0))
    yr, yi = pl.pallas_call(
        _f1_kernel,
        out_shape=(jax.ShapeDtypeStruct((batch, n1, n2, w), F32),) * 2,
        grid=(n2 // F1_TILE, batch),
        in_specs=[
            y_spec,
            pl.BlockSpec((2 * n1 * F1_TILE, n1 * F1_TILE), lambda j, b: (0, 0)),
            tw_spec,
            tw_spec,
        ],
        out_specs=(y_spec, y_spec),
        compiler_params=_params(("arbitrary", "arbitrary")),
        name="fourier1",
    )(x4, fc["k1"], fc["tc"], fc["ts"])
    yr = yr.reshape(batch, n1 * n2, w)
    yi = yi.reshape(batch, n1 * n2, w)
    rows = F2_TILE * n2
    out = pl.pallas_call(
        functools.partial(_f2_kernel, scale=1.0 / math.sqrt(seq * FOURIER_GROUP)),
        out_shape=jax.ShapeDtypeStruct((batch, n2, n1, w), F32),
        grid=(batch, n1 // F2_TILE),
        in_specs=[
            pl.BlockSpec((1, rows, w), lambda b, j: (b, j, 0)),
            pl.BlockSpec((1, rows, w), lambda b, j: (b, j, 0)),
            pl.BlockSpec((2 * n2, n2), lambda b, j: (0, 0)),
            pl.BlockSpec((2 * n2, n2), lambda b, j: (0, 0)),
            pl.BlockSpec((w, w), lambda b, j: (0, 0)),
            pl.BlockSpec((w, w), lambda b, j: (0, 0)),
        ],
        out_specs=pl.BlockSpec((1, n2, F2_TILE, w), lambda b, j: (b, 0, j, 0)),
        scratch_shapes=[pltpu.VMEM((rows, w), BF16), pltpu.VMEM((rows, w), BF16)],
        compiler_params=_params(("arbitrary", "arbitrary")),
        name="fourier2",
    )(yr, yi, fc["ma"], fc["mb"], fc["cc"], fc["cs"])
    return out.reshape(batch * seq, w)


def _log_sigmoid(x):
    return jnp.minimum(x, 0.0) - jnp.log1p(jnp.exp(-jnp.abs(x)))


def _ret_kernel(q_ref, k_ref, v_ref, g_ref, df_ref, db_ref, o_ref, kv_ref, r_ref, *, seq):
    c = RET_CHUNK
    d = RET_HEAD_DIM
    nc = seq // c
    lgf = _log_sigmoid(df_ref[0, 0])
    lgb = _log_sigmoid(db_ref[0, 0])
    ii = lax.broadcasted_iota(jnp.int32, (c, 1), 0).astype(F32)
    kdf = jnp.exp(lgf * (c - 1.0 - ii))
    kdb = jnp.exp(lgb * ii)
    qdf = jnp.exp(lgf * (ii + 1.0))
    qdb = jnp.exp(lgb * (c - ii))
    diff = (lax.broadcasted_iota(jnp.int32, (c, c), 0)
            - lax.broadcasted_iota(jnp.int32, (c, c), 1)).astype(F32)
    lgf_c = jnp.concatenate([lgf] * (c // d), axis=1)
    lgb_c = jnp.concatenate([lgb] * (c // d), axis=1)
    dmat = jnp.where(diff >= 0.0, jnp.exp(lgf_c * jnp.maximum(diff, 0.0)),
                     jnp.exp(lgb_c * jnp.maximum(-diff, 0.0)))
    chunk_f = jnp.exp(lgf * float(c))
    chunk_b = jnp.exp(lgb * float(c))

    def kv_body(n, carry):
        rows = pl.ds(pl.multiple_of(n * c, c), c)
        kc = k_ref[0, rows, :].astype(F32)
        kd = jnp.concatenate([kc * kdf, kc * kdb], axis=1).astype(BF16)
        kv_ref[n] = lax.dot_general(kd, v_ref[0, rows, :], (((0,), (0,)), ((), ())),
                                    preferred_element_type=F32)
        return carry

    lax.fori_loop(0, nc, kv_body, 0, unroll=2)

    def fwd_body(n, state):
        r_ref[n, 0:d, :] = state.astype(BF16)
        return chunk_f * state + kv_ref[n, 0:d, :]

    lax.fori_loop(0, nc, fwd_body, jnp.zeros((d, d), F32))

    def bwd_body(m, state):
        n = nc - 1 - m
        r_ref[n, d:2 * d, :] = state.astype(BF16)
        return chunk_b * state + kv_ref[n, d:2 * d, :]

    lax.fori_loop(0, nc, bwd_body, jnp.zeros((d, d), F32))

    def out_body(t, carry):
        staged = []
        for n in (2 * t, 2 * t + 1):
            rows = pl.ds(pl.multiple_of(n * c, c), c)
            qb = q_ref[0, rows, :]
            s = lax.dot_general(qb, k_ref[0, rows, :], (((1,), (1,)), ((), ())),
                                preferred_element_type=F32)
            qc = qb.astype(F32)
            qd = jnp.concatenate([qc * qdf, qc * qdb], axis=1).astype(BF16)
            staged.append((rows, s, jnp.dot(qd, r_ref[n], preferred_element_type=F32)))
        for rows, s, cross in staged:
            o = jnp.dot((s * dmat).astype(BF16), v_ref[0, rows, :], preferred_element_type=F32) + cross
            mu = jnp.mean(o, axis=-1, keepdims=True)
            var = jnp.mean(jnp.square(o - mu), axis=-1, keepdims=True)
            on = (o - mu) * lax.rsqrt(var + EPS)
            g = g_ref[0, rows, :].astype(F32)
            o_ref[0, rows, :] = (on * (g * jax.nn.sigmoid(g))).astype(BF16)
        return carry

    lax.fori_loop(0, nc // 2, out_body, 0)


def _ret_branch(ret, layer, decay_f, decay_b, batch, seq):
    d = RET_HEAD_DIM
    nc = seq // RET_CHUNK
    r3 = ret.reshape(batch, seq, 4 * BRANCH_WIDTH)

    def sec(k):
        return pl.BlockSpec((1, seq, d), lambda b, h: (b, 0, k * RET_HEADS + h))

    decay_spec = pl.BlockSpec((1, 1, 1, d), lambda b, h: (layer, h, 0, 0))
    out = pl.pallas_call(
        functools.partial(_ret_kernel, seq=seq),
        out_shape=jax.ShapeDtypeStruct((batch, seq, BRANCH_WIDTH), BF16),
        grid=(batch, RET_HEADS),
        in_specs=[sec(0), sec(1), sec(2), sec(3), decay_spec, decay_spec],
        out_specs=pl.BlockSpec((1, seq, d), lambda b, h: (b, 0, h)),
        scratch_shapes=[pltpu.VMEM((nc, 2 * d, d), F32), pltpu.VMEM((nc, 2 * d, d), BF16)],
        compiler_params=_params(("arbitrary", "arbitrary")),
        name="retention",
    )(r3, r3, r3, r3, decay_f, decay_b)
    return out.reshape(batch * seq, BRANCH_WIDTH)


def _na_kernel(q_ref, k0_ref, k1_ref, k2_ref, v0_ref, v1_ref, v2_ref, bias_ref, o_ref,
               kw_ref, vw_ref, *, grid_rows):
    rr = NA_ROWS_PER_STEP
    gw = GRID_W
    blk = rr * gw
    kr = NA_ROWS_MAX
    m = pl.program_id(1)
    for idx, (kref, vref) in enumerate(((k0_ref, v0_ref), (k1_ref, v1_ref), (k2_ref, v2_ref))):
        kw_ref[idx * blk:(idx + 1) * blk, :] = kref[0]
        vw_ref[idx * blk:(idx + 1) * blk, :] = vref[0]
    lane = lax.broadcasted_iota(jnp.int32, (gw, 2 * NA_HEAD_DIM), 1)
    low = lane < NA_HEAD_DIM

    pair_cols = [slice(jp * 2 * NA_HEAD_DIM, (jp + 1) * 2 * NA_HEAD_DIM) for jp in range(NA_HEADS // 2)]

    def window(r):
        r_glob = m * rr + r
        rs = jnp.clip(r_glob - kr // 2, 0, grid_rows - kr)
        krows = pl.ds(pl.multiple_of((rs - (m - 1) * rr) * gw, gw), kr * gw)
        qrows = pl.ds(pl.multiple_of(r * gw, gw), gw)
        return qrows, krows, r_glob - rs

    def scores(r):
        qrows, krows, didx = window(r)
        out = []
        for jp, cols in enumerate(pair_cols):
            q2 = q_ref[0, qrows, cols]
            k2 = kw_ref[krows, cols]
            for e in range(2):
                qm = jnp.where(low if e == 0 else jnp.logical_not(low), q2, jnp.zeros_like(q2))
                s = lax.dot_general(qm, k2, (((1,), (1,)), ((), ())), preferred_element_type=F32)
                out.append(s + bias_ref[0, didx, 2 * jp + e])
        return out

    def softmax(ss):
        out = []
        for s in ss:
            p = jnp.exp(s - jnp.max(s, axis=-1, keepdims=True))
            out.append((p.astype(BF16), jnp.sum(p, axis=-1, keepdims=True)))
        return out

    def values(r, ps):
        qrows, krows, _ = window(r)
        for jp, cols in enumerate(pair_cols):
            v2 = vw_ref[krows, cols]
            outs = [jnp.dot(p, v2, preferred_element_type=F32) / l for p, l in ps[2 * jp:2 * jp + 2]]
            o_ref[0, qrows, cols] = jnp.where(low, outs[0], outs[1]).astype(BF16)

    def group_body(t, carry):
        rows = [NA_ROW_GROUP * t + u for u in range(NA_ROW_GROUP)]
        staged = [scores(r) for r in rows]
        for r, ss in zip(rows, staged):
            values(r, softmax(ss))
        return carry

    lax.fori_loop(0, rr // NA_ROW_GROUP, group_body, 0)


def _na_bias(rpb):
    rpb = rpb.astype(F32)
    per_q = []
    for qc in range(GRID_W):
        cs = min(max(qc - NA_COLS // 2, 0), GRID_W - NA_COLS)
        lo = (NA_COLS - 1) - qc + cs
        seg = rpb[..., lo:lo + NA_COLS]
        per_q.append(jnp.pad(seg, ((0, 0),) * 3 + ((cs, GRID_W - NA_COLS - cs),), constant_values=NEG_INF))
    toep = jnp.stack(per_q, axis=3)
    per_dl = []
    for dl in range(NA_ROWS_MAX):
        lo = NA_ROWS_MAX - 1 - dl
        win = toep[:, :, lo:lo + NA_ROWS_MAX]
        per_dl.append(jnp.transpose(win, (0, 1, 3, 2, 4)).reshape(
            rpb.shape[0], NA_HEADS, GRID_W, NA_ROWS_MAX * GRID_W))
    return jnp.stack(per_dl, axis=1)


def _na_branch(na, layer, bias, batch, seq):
    w = BRANCH_WIDTH
    grid_rows = seq // GRID_W
    rr = NA_ROWS_PER_STEP
    blk = rr * GRID_W
    nblk = grid_rows // rr
    n3 = na.reshape(batch, seq, 3 * w)

    def kv_spec(col, shift):
        return pl.BlockSpec((1, blk, w), lambda b, m: (b, jnp.clip(m + shift, 0, nblk - 1), col))

    out = pl.pallas_call(
        functools.partial(_na_kernel, grid_rows=grid_rows),
        out_shape=jax.ShapeDtypeStruct((batch, seq, w), BF16),
        grid=(batch, nblk),
        in_specs=[pl.BlockSpec((1, blk, w), lambda b, m: (b, m, 0)),
                  kv_spec(1, -1), kv_spec(1, 0), kv_spec(1, 1),
                  kv_spec(2, -1), kv_spec(2, 0), kv_spec(2, 1),
                  pl.BlockSpec((1,) + bias.shape[1:], lambda b, m: (layer, 0, 0, 0, 0))],
        out_specs=pl.BlockSpec((1, blk, w), lambda b, m: (b, m, 0)),
        scratch_shapes=[pltpu.VMEM((3 * blk, w), BF16), pltpu.VMEM((3 * blk, w), BF16)],
        compiler_params=_params(("arbitrary", "arbitrary")),
        name="nbr_attn",
    )(n3, n3, n3, n3, n3, n3, n3, bias)
    return out.reshape(batch * seq, w)


def _merge_kernel(h_ref, gate_ref, xs_ref,
                  br0, br1, br2, br3, wg0, wg1, wg2, wg3, bg0, bg1, bg2, bg3,
                  wb0, wb1, wb2, wb3, wo_ref, o_ref, m_ref, *, nj):
    j = pl.program_id(1)

    @pl.when(j < nj)
    def _():
        acc = None
        for br, wg, bg, wb in ((br0, wg0, bg0, wb0), (br1, wg1, bg1, wb1),
                               (br2, wg2, bg2, wb2), (br3, wg3, bg3, wb3)):
            gate = jax.nn.sigmoid(jnp.dot(h_ref[...], wg[0], preferred_element_type=F32) + bg[0])
            term = gate * jnp.dot(br[...].astype(BF16), wb[0, 0], preferred_element_type=F32)
            acc = term if acc is None else acc + term
        m_ref[j] = acc.astype(BF16)

    @pl.when(j >= nj)
    def _():
        tn = wo_ref.shape[2]
        out = jnp.dot(m_ref[0], wo_ref[0, 0:tn, :], preferred_element_type=F32)
        for kk in range(1, nj):
            out = out + jnp.dot(m_ref[kk], wo_ref[0, kk * tn:(kk + 1) * tn, :], preferred_element_type=F32)
        o_ref[...] = xs_ref[...] + gate_ref[0] * out


def _merge(x2, h, layer, mods, branches, w_gate_t, b_gate, w_branch_t, w_out_t, seq):
    rows, d = x2.shape
    tn = TN_MERGE
    nj = d // tn
    tps = seq // TM

    def first(j):
        return jnp.minimum(j, nj - 1)

    def second(j):
        return jnp.maximum(j - nj, 0)

    def gate_w_spec(b):
        return pl.BlockSpec((1, d, tn), lambda i, j: (layer, 0, b * nj + first(j)))

    def gate_b_spec(b):
        return pl.BlockSpec((1, 1, tn), lambda i, j: (layer, 0, b * nj + first(j)))

    def branch_w_spec(b):
        return pl.BlockSpec((1, 1, BRANCH_WIDTH, tn), lambda i, j: (layer, b, 0, first(j)))

    br_spec = pl.BlockSpec((TM, BRANCH_WIDTH), lambda i, j: (i, 0))
    in_specs = [
        pl.BlockSpec((TM, d), lambda i, j: (i, 0)),
        pl.BlockSpec((1, 1, tn), lambda i, j: ((i // tps) * 6 + 2, 0, second(j))),
        pl.BlockSpec((TM, tn), lambda i, j: (i, second(j))),
        br_spec, br_spec, br_spec, br_spec,
        gate_w_spec(0), gate_w_spec(1), gate_w_spec(2), gate_w_spec(3),
        gate_b_spec(0), gate_b_spec(1), gate_b_spec(2), gate_b_spec(3),
        branch_w_spec(0), branch_w_spec(1), branch_w_spec(2), branch_w_spec(3),
        pl.BlockSpec((1, d, tn), lambda i, j: (layer, 0, second(j))),
    ]
    return pl.pallas_call(
        functools.partial(_merge_kernel, nj=nj),
        out_shape=jax.ShapeDtypeStruct((rows, d), F32),
        grid=(rows // TM, 2 * nj),
        in_specs=in_specs,
        out_specs=pl.BlockSpec((TM, tn), lambda i, j: (i, second(j))),
        scratch_shapes=[pltpu.VMEM((nj, TM, tn), BF16)],
        compiler_params=_params(("arbitrary", "arbitrary")),
        name="merge",
    )(h, mods, x2, *branches,
      w_gate_t, w_gate_t, w_gate_t, w_gate_t, b_gate, b_gate, b_gate, b_gate,
      w_branch_t, w_branch_t, w_branch_t, w_branch_t, w_out_t)


def _ffn_kernel(x_ref, xp_ref, xn_ref, g_ref, sc_ref, sh_ref, gate_ref, xs_ref,
                wa_ref, wb_ref, cwa_ref, cwb_ref, cba_ref, cbb_ref, wd_ref,
                o_ref, h_ref, ua0_ref, ub0_ref, ua1_ref, ub1_ref, act_ref, *, tps, nf):
    i = pl.program_id(0)
    j = pl.program_id(1)
    tm = x_ref.shape[0]
    tf = ua0_ref.shape[1]
    ti = i % tps

    ch = FFN_CHUNK
    ext = ch + 2 * 8

    bufs = ((ua0_ref, ub0_ref), (ua1_ref, ub1_ref))

    def up(slot):
        h = h_ref[...]
        bufs[slot][0][...] = jnp.dot(h, wa_ref[0], preferred_element_type=F32)
        bufs[slot][1][...] = jnp.dot(h, wb_ref[0], preferred_element_type=F32)

    def conv(u_ref, cw_ref, cb_ref, r0):
        u = u_ref[r0 + HALO - 8:r0 + HALO - 8 + ext, :]
        prev = pltpu.roll(u, 1, 0)[8:8 + ch]
        nxt = pltpu.roll(u, ext - 1, 0)[8:8 + ch]
        return (prev * cw_ref[0, 0:1, :] + u[8:8 + ch] * cw_ref[0, 1:2, :]
                + nxt * cw_ref[0, 2:3, :] + cb_ref[0])

    def activate(slot, tile):
        for r0 in range(0, tm, ch):
            a = conv(bufs[slot][0], cwa_ref, cba_ref, r0)
            b = conv(bufs[slot][1], cwb_ref, cbb_ref, r0)
            act_ref[tile, r0:r0 + ch, :] = (jax.nn.gelu(a, approximate=True) * b).astype(BF16)

    def down():
        out = jnp.dot(act_ref[0], wd_ref[0, 0:tf, :], preferred_element_type=F32)
        for kk in range(1, nf):
            out = out + jnp.dot(act_ref[kk], wd_ref[0, kk * tf:(kk + 1) * tf, :],
                                preferred_element_type=F32)
        o_ref[...] = xs_ref[...] + gate_ref[0] * out

    @pl.when(j == 0)
    def _():
        g = g_ref[0]
        sc = sc_ref[0]
        sh = sh_ref[0]
        _norm_rows(xp_ref, h_ref, 0, g, sc, sh, keep=ti != 0)
        _norm_rows(x_ref, h_ref, HALO, g, sc, sh)
        _norm_rows(xn_ref, h_ref, HALO + tm, g, sc, sh, keep=ti != tps - 1)
        up(0)

    for parity in range(2):
        @pl.when((j >= 1) & (j < nf) & (j % 2 == parity))
        def _():
            activate(1 - parity, j - 1)
            up(parity)

    @pl.when(j == nf)
    def _():
        activate((nf - 1) % 2, nf - 1)
        down()

    @pl.when(j > nf)
    def _():
        down()


def _ffn(x2, layer, g, mods, w_up_t, conv_w, conv_b, w_down_t, seq):
    rows, d = x2.shape
    tm = TM_FFN
    tps = seq // tm
    nf = D_FF // TF
    tn = TN_DOWN
    nd = d // tn
    hb = tm // HALO
    nhalo = rows // HALO

    def first(j):
        return jnp.minimum(j, nf - 1)

    def second(j):
        return jnp.maximum(j - nf, 0)

    def lagged(j):
        return jnp.clip(j - 1, 0, nf - 1)

    return pl.pallas_call(
        functools.partial(_ffn_kernel, tps=tps, nf=nf),
        out_shape=jax.ShapeDtypeStruct((rows, d), F32),
        grid=(rows // tm, nf + nd),
        in_specs=[
            pl.BlockSpec((tm, d), lambda i, j: (i, 0)),
            pl.BlockSpec((HALO, d), lambda i, j: (jnp.maximum(i * hb - 1, 0), 0)),
            pl.BlockSpec((HALO, d), lambda i, j: (jnp.minimum((i + 1) * hb, nhalo - 1), 0)),
            pl.BlockSpec((1, 1, d), lambda i, j: (layer, 0, 0)),
            pl.BlockSpec((1, 1, d), lambda i, j: ((i // tps) * 6 + 4, 0, 0)),
            pl.BlockSpec((1, 1, d), lambda i, j: ((i // tps) * 6 + 3, 0, 0)),
            pl.BlockSpec((1, 1, tn), lambda i, j: ((i // tps) * 6 + 5, 0, second(j))),
            pl.BlockSpec((tm, tn), lambda i, j: (i, second(j))),
            pl.BlockSpec((1, d, TF), lambda i, j: (layer, 0, first(j))),
            pl.BlockSpec((1, d, TF), lambda i, j: (layer, 0, nf + first(j))),
            pl.BlockSpec((1, 3, TF), lambda i, j: (layer, 0, lagged(j))),
            pl.BlockSpec((1, 3, TF), lambda i, j: (layer, 0, nf + lagged(j))),
            pl.BlockSpec((1, 1, TF), lambda i, j: (layer, 0, lagged(j))),
            pl.BlockSpec((1, 1, TF), lambda i, j: (layer, 0, nf + lagged(j))),
            pl.BlockSpec((1, D_FF, tn), lambda i, j: (layer, 0, second(j))),
        ],
        out_specs=pl.BlockSpec((tm, tn), lambda i, j: (i, second(j))),
        scratch_shapes=[pltpu.VMEM((tm + 2 * HALO, d), BF16),
                        pltpu.VMEM((tm + 2 * HALO, TF), F32),
                        pltpu.VMEM((tm + 2 * HALO, TF), F32),
                        pltpu.VMEM((tm + 2 * HALO, TF), F32),
                        pltpu.VMEM((tm + 2 * HALO, TF), F32),
                        pltpu.VMEM((nf, tm, TF), BF16)],
        compiler_params=_params(("arbitrary", "arbitrary")),
        name="conv_ffn",
    )(x2, x2, x2, g, mods, mods, mods, x2, w_up_t, w_up_t, conv_w, conv_w, conv_b, conv_b, w_down_t)


def _final_kernel(x_ref, g_ref, o_ref):
    x = x_ref[...]
    ms = jnp.mean(x * x, axis=-1, keepdims=True)
    o_ref[...] = x * lax.rsqrt(ms + EPS) * g_ref[...]


def _final_norm(x2, g_final):
    rows, d = x2.shape
    tr = 256
    return pl.pallas_call(
        _final_kernel,
        out_shape=jax.ShapeDtypeStruct((rows, d), F32),
        grid=(rows // tr,),
        in_specs=[pl.BlockSpec((tr, d), lambda i: (i, 0)), pl.BlockSpec((1, d), lambda i: (0, 0))],
        out_specs=pl.BlockSpec((tr, d), lambda i: (i, 0)),
        compiler_params=_params(("arbitrary",)),
        name="final_norm",
    )(x2, g_final)


def _rope_tables(seq):
    half = RET_HEAD_DIM // 2
    inv = ROPE_BASE ** (-jnp.arange(half, dtype=F32) / half)
    ang = jnp.arange(seq, dtype=F32)[:, None] * inv[None, :]
    cos = jnp.cos(ang)
    sin = jnp.sin(ang)
    return jnp.concatenate([cos, cos], axis=1), jnp.concatenate([-sin, sin], axis=1)


def _trunk(x, mods, wts):
    batch, seq, d = x.shape
    x2 = x.reshape(batch * seq, d)
    cos_t, sin_t = _rope_tables(seq)
    fc = _fourier_consts(seq)
    for l in range(DEPTH):
        m = mods[l]
        pool_in, four_in, ret_in, na_in, h = _in_proj(x2, l, wts["g_mix"], m, cos_t, sin_t, wts["w_in"], seq)
        branches = (
            _pool_branch(pool_in, l, wts["pool_w"], wts["pool_scale"], seq),
            _fourier_branch(four_in, fc, batch, seq),
            _ret_branch(ret_in, l, wts["ret_decay_f"], wts["ret_decay_b"], batch, seq),
            _na_branch(na_in, l, wts["na_bias"], batch, seq),
        )
        x2 = _merge(x2, h, l, m, branches, wts["w_gate"], wts["b_gate"], wts["w_branch"], wts["w_out"], seq)
        x2 = _ffn(x2, l, wts["g_ffn"], m, wts["w_up"], wts["conv_w"], wts["conv_b"], wts["w_down"], seq)
    return _final_norm(x2, wts["g_final"]).reshape(batch, seq, d)


def kernel(x_prompt, x_sample, c_prompt, c_sample, w_ada, b_ada, g_mix, w_in, pool_w, pool_scale,
           ret_decay_f, ret_decay_b, na_rpb, w_branch, w_gate, b_gate, w_out, g_ffn, w_up, conv_w,
           conv_b, w_down, g_final):
    d = D_MODEL
    nb_p = c_prompt.shape[0]
    nb_s = c_sample.shape[0]
    c8 = jnp.concatenate([c_prompt, c_sample, jnp.zeros((8 - nb_p - nb_s, d), F32)], axis=0)
    mod_all = _ada(c8, w_ada, b_ada)
    mods_p = mod_all[:, :nb_p].reshape(DEPTH, nb_p * 6, 1, d)
    mods_s = mod_all[:, nb_p:nb_p + nb_s].reshape(DEPTH, nb_s * 6, 1, d)
    decay_shape = (DEPTH, RET_HEADS, 1, RET_HEAD_DIM)
    wts = dict(
        g_mix=g_mix.reshape(DEPTH, 1, d),
        w_in=w_in.astype(BF16),
        pool_w=pool_w.astype(BF16),
        pool_scale=pool_scale.reshape(DEPTH, 1, BRANCH_WIDTH),
        ret_decay_f=jnp.broadcast_to(ret_decay_f.astype(F32)[:, :, None, None], decay_shape),
        ret_decay_b=jnp.broadcast_to(ret_decay_b.astype(F32)[:, :, None, None], decay_shape),
        na_bias=_na_bias(na_rpb),
        w_branch=w_branch.astype(BF16),
        w_gate=w_gate.astype(BF16),
        b_gate=b_gate.reshape(DEPTH, 1, N_BRANCH * d),
        w_out=w_out.astype(BF16),
        g_ffn=g_ffn.reshape(DEPTH, 1, d),
        w_up=w_up.astype(BF16),
        conv_w=conv_w,
        conv_b=conv_b.reshape(DEPTH, 1, 2 * D_FF),
        w_down=w_down.astype(BF16),
        g_final=g_final.reshape(1, d),
    )
    y_prompt = _trunk(x_prompt, mods_p, wts)
    y_sample = _trunk(x_sample, mods_s, wts)
    return (y_prompt, y_sample)
```

```python
import functools
import math

import numpy as np
import jax
import jax.numpy as jnp
from jax import lax
from jax.experimental import pallas as pl
from jax.experimental.pallas import tpu as pltpu

F32 = jnp.float32
BF16 = jnp.bfloat16

D_MODEL = 2048
DEPTH = 4
GRID_W = 64
BRANCH_WIDTH = 512
N_BRANCH = 4
POOL_WINDOWS = (2, 4, 8, 16)
POOL_GROUP = 128
FOURIER_GROUP = 128
RET_HEADS = 4
RET_HEAD_DIM = 128
ROPE_BASE = 10000.0
NA_HEADS = 8
NA_HEAD_DIM = 64
NA_ROWS_MAX = 8
NA_COLS = 16
D_FF = 5632
EPS = 1e-6
NEG_INF = -1e30
IN_WIDTH = 9 * BRANCH_WIDTH

VMEM_LIMIT_BYTES = 56 * 1024 * 1024
BF16_SUBLANES = 16

TM = 1024
TN_IN = 512
TN_MERGE = 256
TM_FFN = 1024
TF = 512
TN_DOWN = 256
HALO = BF16_SUBLANES
NORM_CHUNK = 32
NORM_UNROLL = 4
FFN_CHUNK = 32
POOL_TILE = 512
POOL_HALO = 8
F_T2 = 128
F1_TILE = 8
F2_TILE = 8
RET_CHUNK = 256
NA_ROWS_PER_STEP = 8
NA_ROW_GROUP = 4


def _params(sem):
    return pltpu.CompilerParams(dimension_semantics=sem, vmem_limit_bytes=VMEM_LIMIT_BYTES)


def _norm_rows(x_ref, dst_ref, dst_off, g, scale, shift, keep=None):
    rows = x_ref.shape[0]
    ch = min(rows, NORM_CHUNK)
    gain = g * (1.0 + scale)

    def body(c, carry):
        r0 = pl.multiple_of(c * ch, ch)
        x = x_ref[pl.ds(r0, ch), :]
        ms = jnp.mean(x * x, axis=-1, keepdims=True)
        h = (x * lax.rsqrt(ms + EPS)) * gain + shift
        if keep is not None:
            h = jnp.where(keep, h, 0.0)
        dst_ref[pl.ds(dst_off + r0, ch), :] = h.astype(BF16)
        return carry

    lax.fori_loop(0, rows // ch, body, 0, unroll=min(NORM_UNROLL, rows // ch))


def _ada_kernel(c_ref, w_ref, b_ref, o_ref):
    c = c_ref[...]
    s = (c * jax.nn.sigmoid(c)).astype(BF16)
    o_ref[0] = jnp.dot(s, w_ref[0].astype(BF16), preferred_element_type=F32) + b_ref[0]


def _ada(c8, w_ada, b_ada):
    depth, d, n = w_ada.shape
    tn = 1024
    return pl.pallas_call(
        _ada_kernel,
        out_shape=jax.ShapeDtypeStruct((depth, 8, n), F32),
        grid=(depth, n // tn),
        in_specs=[
            pl.BlockSpec((8, d), lambda l, j: (0, 0)),
            pl.BlockSpec((1, d, tn), lambda l, j: (l, 0, j)),
            pl.BlockSpec((1, 1, tn), lambda l, j: (l, 0, j)),
        ],
        out_specs=pl.BlockSpec((1, 8, tn), lambda l, j: (l, 0, j)),
        compiler_params=_params(("arbitrary", "arbitrary")),
        name="ada",
    )(c8, w_ada, b_ada.reshape(depth, 1, n))


def _rope_tile(acc, cos, sin, scale):
    parts = []
    for hh in range(acc.shape[1] // RET_HEAD_DIM):
        ch = acc[:, hh * RET_HEAD_DIM:(hh + 1) * RET_HEAD_DIM]
        parts.append((ch * cos + pltpu.roll(ch, RET_HEAD_DIM // 2, 1) * sin) * scale)
    return jnp.concatenate(parts, axis=1)


def _in_kernel(x_ref, g_ref, sc_ref, sh_ref, cos_ref, sin_ref, w_ref,
               pool_ref, four_ref, ret_ref, na_ref, h_ref):
    j = pl.program_id(1)

    @pl.when(j == 0)
    def _():
        _norm_rows(x_ref, h_ref, 0, g_ref[0], sc_ref[0], sh_ref[0])

    def proj():
        return jnp.dot(h_ref[...], w_ref[0], preferred_element_type=F32)

    @pl.when(j == 0)
    def _():
        pool_ref[...] = proj()

    @pl.when(j == 1)
    def _():
        four_ref[...] = proj()

    @pl.when(j == 2)
    def _():
        ret_ref[...] = _rope_tile(proj(), cos_ref[...], sin_ref[...], 1.0).astype(BF16)

    @pl.when(j == 3)
    def _():
        ret_ref[...] = _rope_tile(proj(), cos_ref[...], sin_ref[...], RET_HEAD_DIM ** -0.5).astype(BF16)

    @pl.when((j == 4) | (j == 5))
    def _():
        ret_ref[...] = proj().astype(BF16)

    @pl.when(j == 6)
    def _():
        na_ref[...] = (proj() * (NA_HEAD_DIM ** -0.5)).astype(BF16)

    @pl.when(j >= 7)
    def _():
        na_ref[...] = proj().astype(BF16)


def _in_proj(x2, layer, g, mods, cos_t, sin_t, w_in_t, seq):
    rows, d = x2.shape
    tps = seq // TM
    nj = IN_WIDTH // TN_IN

    def mod_spec(k):
        return pl.BlockSpec((1, 1, d), lambda i, j: ((i // tps) * 6 + k, 0, 0))

    return pl.pallas_call(
        _in_kernel,
        out_shape=(
            jax.ShapeDtypeStruct((rows, BRANCH_WIDTH), F32),
            jax.ShapeDtypeStruct((rows, BRANCH_WIDTH), F32),
            jax.ShapeDtypeStruct((rows, 4 * BRANCH_WIDTH), BF16),
            jax.ShapeDtypeStruct((rows, 3 * BRANCH_WIDTH), BF16),
            jax.ShapeDtypeStruct((rows, d), BF16),
        ),
        grid=(rows // TM, nj),
        in_specs=[
            pl.BlockSpec((TM, d), lambda i, j: (i, 0)),
            pl.BlockSpec((1, 1, d), lambda i, j: (layer, 0, 0)),
            mod_spec(1),
            mod_spec(0),
            pl.BlockSpec((TM, RET_HEAD_DIM), lambda i, j: (i % tps, 0)),
            pl.BlockSpec((TM, RET_HEAD_DIM), lambda i, j: (i % tps, 0)),
            pl.BlockSpec((1, d, TN_IN), lambda i, j: (layer, 0, j)),
        ],
        out_specs=(
            pl.BlockSpec((TM, TN_IN), lambda i, j: (i, 0)),
            pl.BlockSpec((TM, TN_IN), lambda i, j: (i, 0)),
            pl.BlockSpec((TM, TN_IN), lambda i, j: (i, jnp.clip(j - 2, 0, 3))),
            pl.BlockSpec((TM, TN_IN), lambda i, j: (i, jnp.clip(j - 6, 0, 2))),
            pl.BlockSpec((TM, d), lambda i, j: (i, 0)),
        ),
        compiler_params=_params(("arbitrary", "arbitrary")),
        name="in_proj",
    )(x2, g, mods, mods, cos_t, sin_t, w_in_t)


def _pool_kernel(a_ref, ap_ref, an_ref, w_ref, s_ref, o_ref, ext_ref, *, tps, seq):
    tp = a_ref.shape[0]
    ti = pl.program_id(0) % tps
    ext_ref[0:POOL_HALO, :] = jnp.where(ti == 0, 0.0, ap_ref[...])
    ext_ref[POOL_HALO:POOL_HALO + tp, :] = a_ref[...]
    ext_ref[POOL_HALO + tp:2 * POOL_HALO + tp, :] = jnp.where(ti == tps - 1, 0.0, an_ref[...])
    t = ti * tp + lax.broadcasted_iota(jnp.int32, (tp, 1), 0)
    for gi, w in enumerate(POOL_WINDOWS):
        cols = slice(gi * POOL_GROUP, (gi + 1) * POOL_GROUP)
        half = w // 2
        acc = ext_ref[pl.ds(POOL_HALO - half, tp), cols]
        for k in range(-half + 1, half):
            acc = acc + ext_ref[pl.ds(POOL_HALO + k, tp), cols]
        count = (jnp.minimum(t + half, seq) - jnp.maximum(t - half, 0)).astype(F32)
        pooled = acc / count - a_ref[:, cols]
        y = jnp.dot(pooled.astype(BF16), w_ref[0, gi], preferred_element_type=F32) * s_ref[0, :, cols]
        o_ref[:, cols] = y.astype(BF16)


def _pool_branch(a, layer, pool_w, pool_scale, seq):
    rows, w = a.shape
    tp = POOL_TILE
    tps = seq // tp
    hb = tp // POOL_HALO
    nhalo = rows // POOL_HALO
    return pl.pallas_call(
        functools.partial(_pool_kernel, tps=tps, seq=seq),
        out_shape=jax.ShapeDtypeStruct((rows, w), BF16),
        grid=(rows // tp,),
        in_specs=[
            pl.BlockSpec((tp, w), lambda i: (i, 0)),
            pl.BlockSpec((POOL_HALO, w), lambda i: (jnp.maximum(i * hb - 1, 0), 0)),
            pl.BlockSpec((POOL_HALO, w), lambda i: (jnp.minimum((i + 1) * hb, nhalo - 1), 0)),
            pl.BlockSpec((1, len(POOL_WINDOWS), POOL_GROUP, POOL_GROUP), lambda i: (layer, 0, 0, 0)),
            pl.BlockSpec((1, 1, w), lambda i: (layer, 0, 0)),
        ],
        out_specs=pl.BlockSpec((tp, w), lambda i: (i, 0)),
        scratch_shapes=[pltpu.VMEM((tp + 2 * POOL_HALO, w), F32)],
        compiler_params=_params(("arbitrary",)),
        name="pool",
    )(a, a, a, pool_w, pool_scale)


def _f1_kernel(x_ref, k1_ref, tc_ref, ts_ref, yr_ref, yi_ref):
    _, n1, t, w = x_ref.shape
    x = x_ref[0].reshape(n1 * t, w).astype(BF16)
    y = jnp.dot(k1_ref[...], x, preferred_element_type=F32)
    yr = y[:n1 * t]
    yi = y[n1 * t:]
    reps = w // tc_ref.shape[2]
    c = jnp.concatenate([tc_ref[...].reshape(n1 * t, w // reps)] * reps, axis=1)
    s = jnp.concatenate([ts_ref[...].reshape(n1 * t, w // reps)] * reps, axis=1)
    yr_ref[0] = (yr * c + yi * s).reshape(n1, t, w)
    yi_ref[0] = (yi * c - yr * s).reshape(n1, t, w)


def _f2_kernel(yr_ref, yi_ref, ma_ref, mb_ref, cc_ref, cs_ref, o_ref, zr_ref, zi_ref, *, scale):
    n2 = F_T2
    for s in range(F2_TILE):
        rows = slice(s * n2, (s + 1) * n2)
        z = (jnp.dot(ma_ref[...], yr_ref[0, rows, :].astype(BF16), preferred_element_type=F32)
             + jnp.dot(mb_ref[...], yi_ref[0, rows, :].astype(BF16), preferred_element_type=F32))
        zr_ref[rows, :] = z[:n2].astype(BF16)
        zi_ref[rows, :] = z[n2:].astype(BF16)
    o = (jnp.dot(zr_ref[...], cc_ref[...], preferred_element_type=F32)
         + jnp.dot(zi_ref[...], cs_ref[...], preferred_element_type=F32)) * scale
    for s in range(F2_TILE):
        o_ref[0, :, s, :] = o[s * n2:(s + 1) * n2]


def _dft_cos_sin(n):
    k = np.arange(n)
    ang = 2.0 * np.pi * ((k[:, None] * k[None, :]) % n) / n
    return np.cos(ang), np.sin(ang)


def _fourier_consts(seq):
    n2 = F_T2
    n1 = seq // n2
    c1, s1 = _dft_cos_sin(n1)
    c2, s2 = _dft_cos_sin(n2)
    cg, sg = _dft_cos_sin(FOURIER_GROUP)
    eye = np.eye(BRANCH_WIDTH // FOURIER_GROUP)
    k1 = jnp.asarray(np.kron(np.concatenate([c1, -s1], axis=0), np.eye(F1_TILE)), BF16)
    ma = jnp.asarray(np.concatenate([c2, -s2], axis=0), BF16)
    mb = jnp.asarray(np.concatenate([s2, c2], axis=0), BF16)
    cc = jnp.asarray(np.kron(eye, cg), BF16)
    cs = jnp.asarray(np.kron(eye, sg), BF16)
    prod = (jnp.arange(n1, dtype=jnp.int32)[:, None] * jnp.arange(n2, dtype=jnp.int32)[None, :]) % seq
    ang = prod.astype(F32) * (2.0 * math.pi / seq)
    shape = (n1, n2, FOURIER_GROUP)
    tc = jnp.broadcast_to(jnp.cos(ang)[:, :, None], shape)
    ts = jnp.broadcast_to(jnp.sin(ang)[:, :, None], shape)
    return dict(k1=k1, ma=ma, mb=mb, cc=cc, cs=cs, tc=tc, ts=ts)


def _fourier_branch(f, fc, batch, seq):
    w = BRANCH_WIDTH
    n2 = F_T2
    n1 = seq // n2
    x4 = f.reshape(batch, n1, n2, w)
    y_spec = pl.BlockSpec((1, n1, F1_TILE, w), lambda j, b: (b, 0, j, 0))
    tw_spec = pl.BlockSpec((n1, F1_TILE, FOURIER_GROUP), lambda j, b: (0, j, 0))
    yr, yi = pl.pallas_call(
        _f1_kernel,
        out_shape=(jax.ShapeDtypeStruct((batch, n1, n2, w), F32),) * 2,
        grid=(n2 // F1_TILE, batch),
        in_specs=[
            y_spec,
            pl.BlockSpec((2 * n1 * F1_TILE, n1 * F1_TILE), lambda j, b: (0, 0)),
            tw_spec,
            tw_spec,
        ],
        out_specs=(y_spec, y_spec),
        compiler_params=_params(("arbitrary", "arbitrary")),
        name="fourier1",
    )(x4, fc["k1"], fc["tc"], fc["ts"])
    yr = yr.reshape(batch, n1 * n2, w)
    yi = yi.reshape(batch, n1 * n2, w)
    rows = F2_TILE * n2
    out = pl.pallas_call(
        functools.partial(_f2_kernel, scale=1.0 / math.sqrt(seq * FOURIER_GROUP)),
        out_shape=jax.ShapeDtypeStruct((batch, n2, n1, w), F32),
        grid=(batch, n1 // F2_TILE),
        in_specs=[
            pl.BlockSpec((1, rows, w), lambda b, j: (b, j, 0)),
            pl.BlockSpec((1, rows, w), lambda b, j: (b, j, 0)),
            pl.BlockSpec((2 * n2, n2), lambda b, j: (0, 0)),
            pl.BlockSpec((2 * n2, n2), lambda b, j: (0, 0)),
            pl.BlockSpec((w, w), lambda b, j: (0, 0)),
            pl.BlockSpec((w, w), lambda b, j: (0, 0)),
        ],
        out_specs=pl.BlockSpec((1, n2, F2_TILE, w), lambda b, j: (b, 0, j, 0)),
        scratch_shapes=[pltpu.VMEM((rows, w), BF16), pltpu.VMEM((rows, w), BF16)],
        compiler_params=_params(("arbitrary", "arbitrary")),
        name="fourier2",
    )(yr, yi, fc["ma"], fc["mb"], fc["cc"], fc["cs"])
    return out.reshape(batch * seq, w)


def _log_sigmoid(x):
    return jnp.minimum(x, 0.0) - jnp.log1p(jnp.exp(-jnp.abs(x)))


def _ret_kernel(q_ref, k_ref, v_ref, g_ref, df_ref, db_ref, o_ref, kv_ref, r_ref, *, seq):
    c = RET_CHUNK
    d = RET_HEAD_DIM
    nc = seq // c
    lgf = _log_sigmoid(df_ref[0, 0])
    lgb = _log_sigmoid(db_ref[0, 0])
    ii = lax.broadcasted_iota(jnp.int32, (c, 1), 0).astype(F32)
    kdf = jnp.exp(lgf * (c - 1.0 - ii))
    kdb = jnp.exp(lgb * ii)
    qdf = jnp.exp(lgf * (ii + 1.0))
    qdb = jnp.exp(lgb * (c - ii))
    diff = (lax.broadcasted_iota(jnp.int32, (c, c), 0)
            - lax.broadcasted_iota(jnp.int32, (c, c), 1)).astype(F32)
    lgf_c = jnp.concatenate([lgf] * (c // d), axis=1)
    lgb_c = jnp.concatenate([lgb] * (c // d), axis=1)
    dmat = jnp.where(diff >= 0.0, jnp.exp(lgf_c * jnp.maximum(diff, 0.0)),
                     jnp.exp(lgb_c * jnp.maximum(-diff, 0.0)))
    chunk_f = jnp.exp(lgf * float(c))
    chunk_b = jnp.exp(lgb * float(c))

    def kv_body(n, carry):
        rows = pl.ds(pl.multiple_of(n * c, c), c)
        kc = k_ref[0, rows, :].astype(F32)
        kd = jnp.concatenate([kc * kdf, kc * kdb], axis=1).astype(BF16)
        kv_ref[n] = lax.dot_general(kd, v_ref[0, rows, :], (((0,), (0,)), ((), ())),
                                    preferred_element_type=F32)
        return carry

    lax.fori_loop(0, nc, kv_body, 0, unroll=2)

    def fwd_body(n, state):
        r_ref[n, 0:d, :] = state.astype(BF16)
        return chunk_f * state + kv_ref[n, 0:d, :]

    lax.fori_loop(0, nc, fwd_body, jnp.zeros((d, d), F32))

    def bwd_body(m, state):
        n = nc - 1 - m
        r_ref[n, d:2 * d, :] = state.astype(BF16)
        return chunk_b * state + kv_ref[n, d:2 * d, :]

    lax.fori_loop(0, nc, bwd_body, jnp.zeros((d, d), F32))

    def out_body(t, carry):
        staged = []
        for n in (2 * t, 2 * t + 1):
            rows = pl.ds(pl.multiple_of(n * c, c), c)
            qb = q_ref[0, rows, :]
            s = lax.dot_general(qb, k_ref[0, rows, :], (((1,), (1,)), ((), ())),
                                preferred_element_type=F32)
            qc = qb.astype(F32)
            qd = jnp.concatenate([qc * qdf, qc * qdb], axis=1).astype(BF16)
            staged.append((rows, s, jnp.dot(qd, r_ref[n], preferred_element_type=F32)))
        for rows, s, cross in staged:
            o = jnp.dot((s * dmat).astype(BF16), v_ref[0, rows, :], preferred_element_type=F32) + cross
            mu = jnp.mean(o, axis=-1, keepdims=True)
            var = jnp.mean(jnp.square(o - mu), axis=-1, keepdims=True)
            on = (o - mu) * lax.rsqrt(var + EPS)
            g = g_ref[0, rows, :].astype(F32)
            o_ref[0, rows, :] = (on * (g * jax.nn.sigmoid(g))).astype(BF16)
        return carry

    lax.fori_loop(0, nc // 2, out_body, 0)


def _ret_branch(ret, layer, decay_f, decay_b, batch, seq):
    d = RET_HEAD_DIM
    nc = seq // RET_CHUNK
    r3 = ret.reshape(batch, seq, 4 * BRANCH_WIDTH)

    def sec(k):
        return pl.BlockSpec((1, seq, d), lambda b, h: (b, 0, k * RET_HEADS + h))

    decay_spec = pl.BlockSpec((1, 1, 1, d), lambda b, h: (layer, h, 0, 0))
    out = pl.pallas_call(
        functools.partial(_ret_kernel, seq=seq),
        out_shape=jax.ShapeDtypeStruct((batch, seq, BRANCH_WIDTH), BF16),
        grid=(batch, RET_HEADS),
        in_specs=[sec(0), sec(1), sec(2), sec(3), decay_spec, decay_spec],
        out_specs=pl.BlockSpec((1, seq, d), lambda b, h: (b, 0, h)),
        scratch_shapes=[pltpu.VMEM((nc, 2 * d, d), F32), pltpu.VMEM((nc, 2 * d, d), BF16)],
        compiler_params=_params(("arbitrary", "arbitrary")),
        name="retention",
    )(r3, r3, r3, r3, decay_f, decay_b)
    return out.reshape(batch * seq, BRANCH_WIDTH)


def _na_kernel(q_ref, k0_ref, k1_ref, k2_ref, v0_ref, v1_ref, v2_ref, bias_ref, o_ref,
               kw_ref, vw_ref, *, grid_rows):
    rr = NA_ROWS_PER_STEP
    gw = GRID_W
    blk = rr * gw
    kr = NA_ROWS_MAX
    m = pl.program_id(1)
    for idx, (kref, vref) in enumerate(((k0_ref, v0_ref), (k1_ref, v1_ref), (k2_ref, v2_ref))):
        kw_ref[idx * blk:(idx + 1) * blk, :] = kref[0]
        vw_ref[idx * blk:(idx + 1) * blk, :] = vref[0]
    lane = lax.broadcasted_iota(jnp.int32, (gw, 2 * NA_HEAD_DIM), 1)
    low = lane < NA_HEAD_DIM

    pair_cols = [slice(jp * 2 * NA_HEAD_DIM, (jp + 1) * 2 * NA_HEAD_DIM) for jp in range(NA_HEADS // 2)]

    def window(r):
        r_glob = m * rr + r
        rs = jnp.clip(r_glob - kr // 2, 0, grid_rows - kr)
        krows = pl.ds(pl.multiple_of((rs - (m - 1) * rr) * gw, gw), kr * gw)
        qrows = pl.ds(pl.multiple_of(r * gw, gw), gw)
        return qrows, krows, r_glob - rs

    def scores(r):
        qrows, krows, didx = window(r)
        out = []
        for jp, cols in enumerate(pair_cols):
            q2 = q_ref[0, qrows, cols]
            zero = jnp.zeros_like(q2)
            qq = jnp.concatenate([jnp.where(low, q2, zero), jnp.where(low, zero, q2)], axis=0)
            s = lax.dot_general(qq, kw_ref[krows, cols], (((1,), (1,)), ((), ())),
                                preferred_element_type=F32)
            out.append(s + bias_ref[0, didx, jp])
        return out

    def softmax(ss):
        out = []
        for s in ss:
            p = jnp.exp(s - jnp.max(s, axis=-1, keepdims=True))
            out.append((p.astype(BF16), jnp.sum(p, axis=-1, keepdims=True)))
        return out

    def values(r, ps):
        qrows, krows, _ = window(r)
        for (p, l), cols in zip(ps, pair_cols):
            o = jnp.dot(p, vw_ref[krows, cols], preferred_element_type=F32) / l
            o_ref[0, qrows, cols] = jnp.where(low, o[:gw], o[gw:]).astype(BF16)

    def group_body(t, carry):
        rows = [NA_ROW_GROUP * t + u for u in range(NA_ROW_GROUP)]
        staged = [scores(r) for r in rows]
        for r, ss in zip(rows, staged):
            values(r, softmax(ss))
        return carry

    lax.fori_loop(0, rr // NA_ROW_GROUP, group_body, 0)


def _na_bias(rpb):
    rpb = rpb.astype(F32)
    per_q = []
    for qc in range(GRID_W):
        cs = min(max(qc - NA_COLS // 2, 0), GRID_W - NA_COLS)
        lo = (NA_COLS - 1) - qc + cs
        seg = rpb[..., lo:lo + NA_COLS]
        per_q.append(jnp.pad(seg, ((0, 0),) * 3 + ((cs, GRID_W - NA_COLS - cs),), constant_values=NEG_INF))
    toep = jnp.stack(per_q, axis=3)
    per_dl = []
    for dl in range(NA_ROWS_MAX):
        lo = NA_ROWS_MAX - 1 - dl
        win = toep[:, :, lo:lo + NA_ROWS_MAX]
        per_dl.append(jnp.transpose(win, (0, 1, 3, 2, 4)).reshape(
            rpb.shape[0], NA_HEADS, GRID_W, NA_ROWS_MAX * GRID_W))
    table = jnp.stack(per_dl, axis=1)
    return table.reshape(rpb.shape[0], NA_ROWS_MAX, NA_HEADS // 2, 2 * GRID_W, NA_ROWS_MAX * GRID_W)


def _na_branch(na, layer, bias, batch, seq):
    w = BRANCH_WIDTH
    grid_rows = seq // GRID_W
    rr = NA_ROWS_PER_STEP
    blk = rr * GRID_W
    nblk = grid_rows // rr
    n3 = na.reshape(batch, seq, 3 * w)

    def kv_spec(col, shift):
        return pl.BlockSpec((1, blk, w), lambda b, m: (b, jnp.clip(m + shift, 0, nblk - 1), col))

    out = pl.pallas_call(
        functools.partial(_na_kernel, grid_rows=grid_rows),
        out_shape=jax.ShapeDtypeStruct((batch, seq, w), BF16),
        grid=(batch, nblk),
        in_specs=[pl.BlockSpec((1, blk, w), lambda b, m: (b, m, 0)),
                  kv_spec(1, -1), kv_spec(1, 0), kv_spec(1, 1),
                  kv_spec(2, -1), kv_spec(2, 0), kv_spec(2, 1),
                  pl.BlockSpec((1,) + bias.shape[1:], lambda b, m: (layer, 0, 0, 0, 0))],
        out_specs=pl.BlockSpec((1, blk, w), lambda b, m: (b, m, 0)),
        scratch_shapes=[pltpu.VMEM((3 * blk, w), BF16), pltpu.VMEM((3 * blk, w), BF16)],
        compiler_params=_params(("arbitrary", "arbitrary")),
        name="nbr_attn",
    )(n3, n3, n3, n3, n3, n3, n3, bias)
    return out.reshape(batch * seq, w)


def _merge_kernel(h_ref, gate_ref, xs_ref,
                  br0, br1, br2, br3, wg0, wg1, wg2, wg3, bg0, bg1, bg2, bg3,
                  wb0, wb1, wb2, wb3, wo_ref, o_ref, m_ref, *, nj):
    j = pl.program_id(1)

    @pl.when(j < nj)
    def _():
        acc = None
        for br, wg, bg, wb in ((br0, wg0, bg0, wb0), (br1, wg1, bg1, wb1),
                               (br2, wg2, bg2, wb2), (br3, wg3, bg3, wb3)):
            gate = jax.nn.sigmoid(jnp.dot(h_ref[...], wg[0], preferred_element_type=F32) + bg[0])
            term = gate * jnp.dot(br[...].astype(BF16), wb[0, 0], preferred_element_type=F32)
            acc = term if acc is None else acc + term
        m_ref[j] = acc.astype(BF16)

    @pl.when(j >= nj)
    def _():
        tn = wo_ref.shape[2]
        out = jnp.dot(m_ref[0], wo_ref[0, 0:tn, :], preferred_element_type=F32)
        for kk in range(1, nj):
            out = out + jnp.dot(m_ref[kk], wo_ref[0, kk * tn:(kk + 1) * tn, :], preferred_element_type=F32)
        o_ref[...] = xs_ref[...] + gate_ref[0] * out


def _merge(x2, h, layer, mods, branches, w_gate_t, b_gate, w_branch_t, w_out_t, seq):
    rows, d = x2.shape
    tn = TN_MERGE
    nj = d // tn
    tps = seq // TM

    def first(j):
        return jnp.minimum(j, nj - 1)

    def second(j):
        return jnp.maximum(j - nj, 0)

    def gate_w_spec(b):
        return pl.BlockSpec((1, d, tn), lambda i, j: (layer, 0, b * nj + first(j)))

    def gate_b_spec(b):
        return pl.BlockSpec((1, 1, tn), lambda i, j: (layer, 0, b * nj + first(j)))

    def branch_w_spec(b):
        return pl.BlockSpec((1, 1, BRANCH_WIDTH, tn), lambda i, j: (layer, b, 0, first(j)))

    br_spec = pl.BlockSpec((TM, BRANCH_WIDTH), lambda i, j: (i, 0))
    in_specs = [
        pl.BlockSpec((TM, d), lambda i, j: (i, 0)),
        pl.BlockSpec((1, 1, tn), lambda i, j: ((i // tps) * 6 + 2, 0, second(j))),
        pl.BlockSpec((TM, tn), lambda i, j: (i, second(j))),
        br_spec, br_spec, br_spec, br_spec,
        gate_w_spec(0), gate_w_spec(1), gate_w_spec(2), gate_w_spec(3),
        gate_b_spec(0), gate_b_spec(1), gate_b_spec(2), gate_b_spec(3),
        branch_w_spec(0), branch_w_spec(1), branch_w_spec(2), branch_w_spec(3),
        pl.BlockSpec((1, d, tn), lambda i, j: (layer, 0, second(j))),
    ]
    return pl.pallas_call(
        functools.partial(_merge_kernel, nj=nj),
        out_shape=jax.ShapeDtypeStruct((rows, d), F32),
        grid=(rows // TM, 2 * nj),
        in_specs=in_specs,
        out_specs=pl.BlockSpec((TM, tn), lambda i, j: (i, second(j))),
        scratch_shapes=[pltpu.VMEM((nj, TM, tn), BF16)],
        compiler_params=_params(("arbitrary", "arbitrary")),
        name="merge",
    )(h, mods, x2, *branches,
      w_gate_t, w_gate_t, w_gate_t, w_gate_t, b_gate, b_gate, b_gate, b_gate,
      w_branch_t, w_branch_t, w_branch_t, w_branch_t, w_out_t)


def _ffn_kernel(x_ref, xp_ref, xn_ref, g_ref, sc_ref, sh_ref, gate_ref, xs_ref,
                wa_ref, wb_ref, ca_ref, cb_ref, wd_ref,
                o_ref, h_ref, ua0_ref, ub0_ref, ua1_ref, ub1_ref, act_ref, *, tps, nf):
    i = pl.program_id(0)
    j = pl.program_id(1)
    tm = x_ref.shape[0]
    tf = ua0_ref.shape[1]
    ti = i % tps

    ch = FFN_CHUNK
    ext = ch + 2 * 8

    bufs = ((ua0_ref, ub0_ref), (ua1_ref, ub1_ref))

    def up(slot):
        h = h_ref[...]
        bufs[slot][0][...] = jnp.dot(h, wa_ref[0], preferred_element_type=F32)
        bufs[slot][1][...] = jnp.dot(h, wb_ref[0], preferred_element_type=F32)

    def conv(u_ref, c_ref, r0):
        u = u_ref[r0 + HALO - 8:r0 + HALO - 8 + ext, :]
        prev = pltpu.roll(u, 1, 0)[8:8 + ch]
        nxt = pltpu.roll(u, ext - 1, 0)[8:8 + ch]
        return (prev * c_ref[0, 0:1, :] + u[8:8 + ch] * c_ref[0, 1:2, :]
                + nxt * c_ref[0, 2:3, :] + c_ref[0, 3:4, :])

    def activate(slot, tile):
        for r0 in range(0, tm, ch):
            a = conv(bufs[slot][0], ca_ref, r0)
            b = conv(bufs[slot][1], cb_ref, r0)
            act_ref[tile, r0:r0 + ch, :] = (jax.nn.gelu(a, approximate=True) * b).astype(BF16)

    def down():
        out = jnp.dot(act_ref[0], wd_ref[0, 0:tf, :], preferred_element_type=F32)
        for kk in range(1, nf):
            out = out + jnp.dot(act_ref[kk], wd_ref[0, kk * tf:(kk + 1) * tf, :],
                                preferred_element_type=F32)
        o_ref[...] = xs_ref[...] + gate_ref[0] * out

    @pl.when(j == 0)
    def _():
        g = g_ref[0]
        sc = sc_ref[0]
        sh = sh_ref[0]
        _norm_rows(xp_ref, h_ref, 0, g, sc, sh, keep=ti != 0)
        _norm_rows(x_ref, h_ref, HALO, g, sc, sh)
        _norm_rows(xn_ref, h_ref, HALO + tm, g, sc, sh, keep=ti != tps - 1)
        up(0)

    for parity in range(2):
        @pl.when((j >= 1) & (j < nf) & (j % 2 == parity))
        def _():
            activate(1 - parity, j - 1)
            up(parity)

    @pl.when(j == nf)
    def _():
        activate((nf - 1) % 2, nf - 1)
        down()

    @pl.when(j > nf)
    def _():
        down()


def _ffn(x2, layer, g, mods, w_up_t, conv_p, w_down_t, seq):
    rows, d = x2.shape
    tm = TM_FFN
    tps = seq // tm
    nf = D_FF // TF
    tn = TN_DOWN
    nd = d // tn
    hb = tm // HALO
    nhalo = rows // HALO

    def first(j):
        return jnp.minimum(j, nf - 1)

    def second(j):
        return jnp.maximum(j - nf, 0)

    def lagged(j):
        return jnp.clip(j - 1, 0, nf - 1)

    return pl.pallas_call(
        functools.partial(_ffn_kernel, tps=tps, nf=nf),
        out_shape=jax.ShapeDtypeStruct((rows, d), F32),
        grid=(rows // tm, nf + nd),
        in_specs=[
            pl.BlockSpec((tm, d), lambda i, j: (i, 0), pipeline_mode=pl.Buffered(1)),
            pl.BlockSpec((HALO, d), lambda i, j: (jnp.maximum(i * hb - 1, 0), 0)),
            pl.BlockSpec((HALO, d), lambda i, j: (jnp.minimum((i + 1) * hb, nhalo - 1), 0)),
            pl.BlockSpec((1, 1, d), lambda i, j: (layer, 0, 0)),
            pl.BlockSpec((1, 1, d), lambda i, j: ((i // tps) * 6 + 4, 0, 0)),
            pl.BlockSpec((1, 1, d), lambda i, j: ((i // tps) * 6 + 3, 0, 0)),
            pl.BlockSpec((1, 1, tn), lambda i, j: ((i // tps) * 6 + 5, 0, second(j))),
            pl.BlockSpec((tm, tn), lambda i, j: (i, second(j))),
            pl.BlockSpec((1, d, TF), lambda i, j: (layer, 0, first(j))),
            pl.BlockSpec((1, d, TF), lambda i, j: (layer, 0, nf + first(j))),
            pl.BlockSpec((1, 8, TF), lambda i, j: (layer, 0, lagged(j))),
            pl.BlockSpec((1, 8, TF), lambda i, j: (layer, 0, nf + lagged(j))),
            pl.BlockSpec((1, D_FF, tn), lambda i, j: (layer, 0, second(j))),
        ],
        out_specs=pl.BlockSpec((tm, tn), lambda i, j: (i, second(j))),
        scratch_shapes=[pltpu.VMEM((tm + 2 * HALO, d), BF16),
                        pltpu.VMEM((tm + 2 * HALO, TF), F32),
                        pltpu.VMEM((tm + 2 * HALO, TF), F32),
                        pltpu.VMEM((tm + 2 * HALO, TF), F32),
                        pltpu.VMEM((tm + 2 * HALO, TF), F32),
                        pltpu.VMEM((nf, tm, TF), BF16)],
        compiler_params=_params(("arbitrary", "arbitrary")),
        name="conv_ffn",
    )(x2, x2, x2, g, mods, mods, mods, x2, w_up_t, w_up_t, conv_p, conv_p, w_down_t)


def _final_kernel(x_ref, g_ref, o_ref):
    x = x_ref[...]
    ms = jnp.mean(x * x, axis=-1, keepdims=True)
    o_ref[...] = x * lax.rsqrt(ms + EPS) * g_ref[...]


def _final_norm(x2, g_final):
    rows, d = x2.shape
    tr = 256
    return pl.pallas_call(
        _final_kernel,
        out_shape=jax.ShapeDtypeStruct((rows, d), F32),
        grid=(rows // tr,),
        in_specs=[pl.BlockSpec((tr, d), lambda i: (i, 0)), pl.BlockSpec((1, d), lambda i: (0, 0))],
        out_specs=pl.BlockSpec((tr, d), lambda i: (i, 0)),
        compiler_params=_params(("arbitrary",)),
        name="final_norm",
    )(x2, g_final)


def _rope_tables(seq):
    half = RET_HEAD_DIM // 2
    inv = ROPE_BASE ** (-jnp.arange(half, dtype=F32) / half)
    ang = jnp.arange(seq, dtype=F32)[:, None] * inv[None, :]
    cos = jnp.cos(ang)
    sin = jnp.sin(ang)
    return jnp.concatenate([cos, cos], axis=1), jnp.concatenate([-sin, sin], axis=1)


def _trunk(x, mods, wts):
    batch, seq, d = x.shape
    x2 = x.reshape(batch * seq, d)
    cos_t, sin_t = _rope_tables(seq)
    fc = _fourier_consts(seq)
    for l in range(DEPTH):
        m = mods[l]
        pool_in, four_in, ret_in, na_in, h = _in_proj(x2, l, wts["g_mix"], m, cos_t, sin_t, wts["w_in"], seq)
        branches = (
            _pool_branch(pool_in, l, wts["pool_w"], wts["pool_scale"], seq),
            _fourier_branch(four_in, fc, batch, seq),
            _ret_branch(ret_in, l, wts["ret_decay_f"], wts["ret_decay_b"], batch, seq),
            _na_branch(na_in, l, wts["na_bias"], batch, seq),
        )
        x2 = _merge(x2, h, l, m, branches, wts["w_gate"], wts["b_gate"], wts["w_branch"], wts["w_out"], seq)
        x2 = _ffn(x2, l, wts["g_ffn"], m, wts["w_up"], wts["conv_p"], wts["w_down"], seq)
    return _final_norm(x2, wts["g_final"]).reshape(batch, seq, d)


def kernel(x_prompt, x_sample, c_prompt, c_sample, w_ada, b_ada, g_mix, w_in, pool_w, pool_scale,
           ret_decay_f, ret_decay_b, na_rpb, w_branch, w_gate, b_gate, w_out, g_ffn, w_up, conv_w,
           conv_b, w_down, g_final):
    d = D_MODEL
    nb_p = c_prompt.shape[0]
    nb_s = c_sample.shape[0]
    c8 = jnp.concatenate([c_prompt, c_sample, jnp.zeros((8 - nb_p - nb_s, d), F32)], axis=0)
    mod_all = _ada(c8, w_ada, b_ada)
    mods_p = mod_all[:, :nb_p].reshape(DEPTH, nb_p * 6, 1, d)
    mods_s = mod_all[:, nb_p:nb_p + nb_s].reshape(DEPTH, nb_s * 6, 1, d)
    decay_shape = (DEPTH, RET_HEADS, 1, RET_HEAD_DIM)
    wts = dict(
        g_mix=g_mix.reshape(DEPTH, 1, d),
        w_in=w_in.astype(BF16),
        pool_w=pool_w.astype(BF16),
        pool_scale=pool_scale.reshape(DEPTH, 1, BRANCH_WIDTH),
        ret_decay_f=jnp.broadcast_to(ret_decay_f.astype(F32)[:, :, None, None], decay_shape),
        ret_decay_b=jnp.broadcast_to(ret_decay_b.astype(F32)[:, :, None, None], decay_shape),
        na_bias=_na_bias(na_rpb),
        w_branch=w_branch.astype(BF16),
        w_gate=w_gate.astype(BF16),
        b_gate=b_gate.reshape(DEPTH, 1, N_BRANCH * d),
        w_out=w_out.astype(BF16),
        g_ffn=g_ffn.reshape(DEPTH, 1, d),
        w_up=w_up.astype(BF16),
        conv_p=jnp.concatenate([conv_w, conv_b[:, None, :], jnp.zeros((DEPTH, 4, 2 * D_FF), F32)], axis=1),
        w_down=w_down.astype(BF16),
        g_final=g_final.reshape(1, d),
    )
    y_prompt = _trunk(x_prompt, mods_p, wts)
    y_sample = _trunk(x_sample, mods_s, wts)
    return (y_prompt, y_sample)
```

```python
import functools
import math

import numpy as np
import jax
import jax.numpy as jnp
from jax import lax
from jax.experimental import pallas as pl
from jax.experimental.pallas import tpu as pltpu

F32 = jnp.float32
BF16 = jnp.bfloat16

D_MODEL = 2048
DEPTH = 4
GRID_W = 64
BRANCH_WIDTH = 512
N_BRANCH = 4
POOL_WINDOWS = (2, 4, 8, 16)
POOL_GROUP = 128
FOURIER_GROUP = 128
RET_HEADS = 4
RET_HEAD_DIM = 128
ROPE_BASE = 10000.0
NA_HEADS = 8
NA_HEAD_DIM = 64
NA_ROWS_MAX = 8
NA_COLS = 16
D_FF = 5632
EPS = 1e-6
NEG_INF = -1e30
IN_WIDTH = 9 * BRANCH_WIDTH

VMEM_LIMIT_BYTES = 56 * 1024 * 1024
BF16_SUBLANES = 16

TM = 1024
TN_IN = 512
TN_MERGE = 256
TM_FFN = 1024
TF = 512
TN_DOWN = 256
HALO = BF16_SUBLANES
NORM_CHUNK = 32
NORM_UNROLL = 4
FFN_CHUNK = 32
POOL_TILE = 512
POOL_HALO = 8
F_T2 = 128
F1_TILE = 8
F2_TILE = 8
RET_CHUNK = 256
NA_ROWS_PER_STEP = 8
NA_ROW_GROUP = 4


def _params(sem):
    return pltpu.CompilerParams(dimension_semantics=sem, vmem_limit_bytes=VMEM_LIMIT_BYTES)


def _norm_rows(x_ref, dst_ref, dst_off, g, scale, shift, keep=None):
    rows = x_ref.shape[0]
    ch = min(rows, NORM_CHUNK)
    gain = g * (1.0 + scale)

    def body(c, carry):
        r0 = pl.multiple_of(c * ch, ch)
        x = x_ref[pl.ds(r0, ch), :]
        ms = jnp.mean(x * x, axis=-1, keepdims=True)
        h = (x * lax.rsqrt(ms + EPS)) * gain + shift
        if keep is not None:
            h = jnp.where(keep, h, 0.0)
        dst_ref[pl.ds(dst_off + r0, ch), :] = h.astype(BF16)
        return carry

    lax.fori_loop(0, rows // ch, body, 0, unroll=min(NORM_UNROLL, rows // ch))


def _col_tiles(w, tn):
    depth, k, n = w.shape
    return w.astype(BF16).reshape(depth, k, n // tn, tn).transpose(0, 2, 1, 3)


def _ada_kernel(c_ref, w_ref, b_ref, o_ref):
    c = c_ref[...]
    s = (c * jax.nn.sigmoid(c)).astype(BF16)
    o_ref[0] = jnp.dot(s, w_ref[0].astype(BF16), preferred_element_type=F32) + b_ref[0]


def _ada(c8, w_ada, b_ada):
    depth, d, n = w_ada.shape
    tn = 1024
    return pl.pallas_call(
        _ada_kernel,
        out_shape=jax.ShapeDtypeStruct((depth, 8, n), F32),
        grid=(depth, n // tn),
        in_specs=[
            pl.BlockSpec((8, d), lambda l, j: (0, 0)),
            pl.BlockSpec((1, d, tn), lambda l, j: (l, 0, j)),
            pl.BlockSpec((1, 1, tn), lambda l, j: (l, 0, j)),
        ],
        out_specs=pl.BlockSpec((1, 8, tn), lambda l, j: (l, 0, j)),
        compiler_params=_params(("arbitrary", "arbitrary")),
        name="ada",
    )(c8, w_ada, b_ada.reshape(depth, 1, n))


def _rope_tile(acc, cos, sin, scale):
    parts = []
    for hh in range(acc.shape[1] // RET_HEAD_DIM):
        ch = acc[:, hh * RET_HEAD_DIM:(hh + 1) * RET_HEAD_DIM]
        parts.append((ch * cos + pltpu.roll(ch, RET_HEAD_DIM // 2, 1) * sin) * scale)
    return jnp.concatenate(parts, axis=1)


def _in_kernel(xa_ref, xb_ref, g_ref, sc_ref, sh_ref, cos_ref, sin_ref, w_ref,
               pool_ref, four_ref, ret_ref, na_ref, h_ref):
    j = pl.program_id(1)

    @pl.when(j == 0)
    def _():
        _norm_rows(xa_ref, h_ref, 0, g_ref[0], sc_ref[0], sh_ref[0])
        _norm_rows(xb_ref, h_ref, xa_ref.shape[0], g_ref[0], sc_ref[0], sh_ref[0])

    def proj():
        return jnp.dot(h_ref[...], w_ref[0], preferred_element_type=F32)

    @pl.when(j == 0)
    def _():
        pool_ref[...] = proj()

    @pl.when(j == 1)
    def _():
        four_ref[...] = proj()

    @pl.when(j == 2)
    def _():
        ret_ref[...] = _rope_tile(proj(), cos_ref[...], sin_ref[...], 1.0).astype(BF16)

    @pl.when(j == 3)
    def _():
        ret_ref[...] = _rope_tile(proj(), cos_ref[...], sin_ref[...], RET_HEAD_DIM ** -0.5).astype(BF16)

    @pl.when((j == 4) | (j == 5))
    def _():
        ret_ref[...] = proj().astype(BF16)

    @pl.when(j == 6)
    def _():
        na_ref[...] = (proj() * (NA_HEAD_DIM ** -0.5)).astype(BF16)

    @pl.when(j >= 7)
    def _():
        na_ref[...] = proj().astype(BF16)


def _in_proj(x2, layer, g, mods, cos_t, sin_t, w_in_t, seq):
    rows, d = x2.shape
    tps = seq // TM
    nj = IN_WIDTH // TN_IN
    last_half = 2 * (rows // TM)

    def mod_spec(k):
        return pl.BlockSpec((1, 1, d), lambda i, j: ((i // tps) * 6 + k, 0, 0))

    return pl.pallas_call(
        _in_kernel,
        out_shape=(
            jax.ShapeDtypeStruct((rows, BRANCH_WIDTH), F32),
            jax.ShapeDtypeStruct((rows, BRANCH_WIDTH), F32),
            jax.ShapeDtypeStruct((rows, 4 * BRANCH_WIDTH), BF16),
            jax.ShapeDtypeStruct((rows, 3 * BRANCH_WIDTH), BF16),
            jax.ShapeDtypeStruct((rows, d), BF16),
        ),
        grid=(rows // TM, nj),
        in_specs=[
            pl.BlockSpec((TM // 2, d), lambda i, j: (jnp.minimum(2 * (i + j // (nj - 1)), last_half - 2), 0)),
            pl.BlockSpec((TM // 2, d), lambda i, j: (2 * i + 1, 0)),
            pl.BlockSpec((1, 1, d), lambda i, j: (layer, 0, 0)),
            mod_spec(1),
            mod_spec(0),
            pl.BlockSpec((TM, RET_HEAD_DIM), lambda i, j: (i % tps, 0)),
            pl.BlockSpec((TM, RET_HEAD_DIM), lambda i, j: (i % tps, 0)),
            pl.BlockSpec((1, d, TN_IN), lambda i, j: (layer, 0, j)),
        ],
        out_specs=(
            pl.BlockSpec((TM, TN_IN), lambda i, j: (i, 0)),
            pl.BlockSpec((TM, TN_IN), lambda i, j: (i, 0)),
            pl.BlockSpec((TM, TN_IN), lambda i, j: (i, jnp.clip(j - 2, 0, 3))),
            pl.BlockSpec((TM, TN_IN), lambda i, j: (i, jnp.clip(j - 6, 0, 2))),
            pl.BlockSpec((TM, d), lambda i, j: (i, 0)),
        ),
        compiler_params=_params(("arbitrary", "arbitrary")),
        name="in_proj",
    )(x2, x2, g, mods, mods, cos_t, sin_t, w_in_t)


def _pool_kernel(a_ref, ap_ref, an_ref, w_ref, s_ref, o_ref, ext_ref, *, tps, seq):
    tp = a_ref.shape[0]
    ti = pl.program_id(0) % tps
    ext_ref[0:POOL_HALO, :] = jnp.where(ti == 0, 0.0, ap_ref[...])
    ext_ref[POOL_HALO:POOL_HALO + tp, :] = a_ref[...]
    ext_ref[POOL_HALO + tp:2 * POOL_HALO + tp, :] = jnp.where(ti == tps - 1, 0.0, an_ref[...])
    t = ti * tp + lax.broadcasted_iota(jnp.int32, (tp, 1), 0)
    for gi, w in enumerate(POOL_WINDOWS):
        cols = slice(gi * POOL_GROUP, (gi + 1) * POOL_GROUP)
        half = w // 2
        acc = ext_ref[pl.ds(POOL_HALO - half, tp), cols]
        for k in range(-half + 1, half):
            acc = acc + ext_ref[pl.ds(POOL_HALO + k, tp), cols]
        count = (jnp.minimum(t + half, seq) - jnp.maximum(t - half, 0)).astype(F32)
        pooled = acc / count - a_ref[:, cols]
        y = jnp.dot(pooled.astype(BF16), w_ref[0, gi], preferred_element_type=F32) * s_ref[0, :, cols]
        o_ref[:, cols] = y.astype(BF16)


def _pool_branch(a, layer, pool_w, pool_scale, seq):
    rows, w = a.shape
    tp = POOL_TILE
    tps = seq // tp
    hb = tp // POOL_HALO
    nhalo = rows // POOL_HALO
    return pl.pallas_call(
        functools.partial(_pool_kernel, tps=tps, seq=seq),
        out_shape=jax.ShapeDtypeStruct((rows, w), BF16),
        grid=(rows // tp,),
        in_specs=[
            pl.BlockSpec((tp, w), lambda i: (i, 0)),
            pl.BlockSpec((POOL_HALO, w), lambda i: (jnp.maximum(i * hb - 1, 0), 0)),
            pl.BlockSpec((POOL_HALO, w), lambda i: (jnp.minimum((i + 1) * hb, nhalo - 1), 0)),
            pl.BlockSpec((1, len(POOL_WINDOWS), POOL_GROUP, POOL_GROUP), lambda i: (layer, 0, 0, 0)),
            pl.BlockSpec((1, 1, w), lambda i: (layer, 0, 0)),
        ],
        out_specs=pl.BlockSpec((tp, w), lambda i: (i, 0)),
        scratch_shapes=[pltpu.VMEM((tp + 2 * POOL_HALO, w), F32)],
        compiler_params=_params(("arbitrary",)),
        name="pool",
    )(a, a, a, pool_w, pool_scale)


def _f1_kernel(x_ref, k1_ref, tc_ref, ts_ref, yr_ref, yi_ref):
    _, n1, t, w = x_ref.shape
    x = x_ref[0].reshape(n1 * t, w).astype(BF16)
    y = jnp.dot(k1_ref[...], x, preferred_element_type=F32)
    yr = y[:n1 * t]
    yi = y[n1 * t:]
    reps = w // tc_ref.shape[2]
    c = jnp.concatenate([tc_ref[...].reshape(n1 * t, w // reps)] * reps, axis=1)
    s = jnp.concatenate([ts_ref[...].reshape(n1 * t, w // reps)] * reps, axis=1)
    yr_ref[0] = (yr * c + yi * s).reshape(n1, t, w)
    yi_ref[0] = (yi * c - yr * s).reshape(n1, t, w)


def _f2_kernel(yr_ref, yi_ref, ma_ref, mb_ref, cc_ref, cs_ref, o_ref, zr_ref, zi_ref, *, scale):
    n2 = F_T2
    for s in range(F2_TILE):
        rows = slice(s * n2, (s + 1) * n2)
        z = (jnp.dot(ma_ref[...], yr_ref[0, rows, :].astype(BF16), preferred_element_type=F32)
             + jnp.dot(mb_ref[...], yi_ref[0, rows, :].astype(BF16), preferred_element_type=F32))
        zr_ref[rows, :] = z[:n2].astype(BF16)
        zi_ref[rows, :] = z[n2:].astype(BF16)
    o = (jnp.dot(zr_ref[...], cc_ref[...], preferred_element_type=F32)
         + jnp.dot(zi_ref[...], cs_ref[...], preferred_element_type=F32)) * scale
    for s in range(F2_TILE):
        o_ref[0, :, s, :] = o[s * n2:(s + 1) * n2]


def _dft_cos_sin(n):
    k = np.arange(n)
    ang = 2.0 * np.pi * ((k[:, None] * k[None, :]) % n) / n
    return np.cos(ang), np.sin(ang)


def _fourier_consts(seq):
    n2 = F_T2
    n1 = seq // n2
    c1, s1 = _dft_cos_sin(n1)
    c2, s2 = _dft_cos_sin(n2)
    cg, sg = _dft_cos_sin(FOURIER_GROUP)
    eye = np.eye(BRANCH_WIDTH // FOURIER_GROUP)
    k1 = jnp.asarray(np.kron(np.concatenate([c1, -s1], axis=0), np.eye(F1_TILE)), BF16)
    ma = jnp.asarray(np.concatenate([c2, -s2], axis=0), BF16)
    mb = jnp.asarray(np.concatenate([s2, c2], axis=0), BF16)
    cc = jnp.asarray(np.kron(eye, cg), BF16)
    cs = jnp.asarray(np.kron(eye, sg), BF16)
    prod = (jnp.arange(n1, dtype=jnp.int32)[:, None] * jnp.arange(n2, dtype=jnp.int32)[None, :]) % seq
    ang = prod.astype(F32) * (2.0 * math.pi / seq)
    shape = (n1, n2, FOURIER_GROUP)
    tc = jnp.broadcast_to(jnp.cos(ang)[:, :, None], shape)
    ts = jnp.broadcast_to(jnp.sin(ang)[:, :, None], shape)
    return dict(k1=k1, ma=ma, mb=mb, cc=cc, cs=cs, tc=tc, ts=ts)


def _fourier_branch(f, fc, batch, seq):
    w = BRANCH_WIDTH
    n2 = F_T2
    n1 = seq // n2
    x4 = f.reshape(batch, n1, n2, w)
    y_spec = pl.BlockSpec((1, n1, F1_TILE, w), lambda j, b: (b, 0, j, 0))
    tw_spec = pl.BlockSpec((n1, F1_TILE, FOURIER_GROUP), lambda j, b: (0, j, 0))
    yr, yi = pl.pallas_call(
        _f1_kernel,
        out_shape=(jax.ShapeDtypeStruct((batch, n1, n2, w), F32),) * 2,
        grid=(n2 // F1_TILE, batch),
        in_specs=[
            y_spec,
            pl.BlockSpec((2 * n1 * F1_TILE, n1 * F1_TILE), lambda j, b: (0, 0)),
            tw_spec,
            tw_spec,
        ],
        out_specs=(y_spec, y_spec),
        compiler_params=_params(("arbitrary", "arbitrary")),
        name="fourier1",
    )(x4, fc["k1"], fc["tc"], fc["ts"])
    yr = yr.reshape(batch, n1 * n2, w)
    yi = yi.reshape(batch, n1 * n2, w)
    rows = F2_TILE * n2
    out = pl.pallas_call(
        functools.partial(_f2_kernel, scale=1.0 / math.sqrt(seq * FOURIER_GROUP)),
        out_shape=jax.ShapeDtypeStruct((batch, n2, n1, w), F32),
        grid=(batch, n1 // F2_TILE),
        in_specs=[
            pl.BlockSpec((1, rows, w), lambda b, j: (b, j, 0)),
            pl.BlockSpec((1, rows, w), lambda b, j: (b, j, 0)),
            pl.BlockSpec((2 * n2, n2), lambda b, j: (0, 0)),
            pl.BlockSpec((2 * n2, n2), lambda b, j: (0, 0)),
            pl.BlockSpec((w, w), lambda b, j: (0, 0)),
            pl.BlockSpec((w, w), lambda b, j: (0, 0)),
        ],
        out_specs=pl.BlockSpec((1, n2, F2_TILE, w), lambda b, j: (b, 0, j, 0)),
        scratch_shapes=[pltpu.VMEM((rows, w), BF16), pltpu.VMEM((rows, w), BF16)],
        compiler_params=_params(("arbitrary", "arbitrary")),
        name="fourier2",
    )(yr, yi, fc["ma"], fc["mb"], fc["cc"], fc["cs"])
    return out.reshape(batch * seq, w)


def _log_sigmoid(x):
    return jnp.minimum(x, 0.0) - jnp.log1p(jnp.exp(-jnp.abs(x)))


def _ret_kernel(q_ref, k_ref, v_ref, g_ref, df_ref, db_ref, o_ref, kv_ref, r_ref, *, seq):
    c = RET_CHUNK
    d = RET_HEAD_DIM
    nc = seq // c
    lgf = _log_sigmoid(df_ref[0, 0])
    lgb = _log_sigmoid(db_ref[0, 0])
    ii = lax.broadcasted_iota(jnp.int32, (c, 1), 0).astype(F32)
    kdf = jnp.exp(lgf * (c - 1.0 - ii))
    kdb = jnp.exp(lgb * ii)
    qdf = jnp.exp(lgf * (ii + 1.0))
    qdb = jnp.exp(lgb * (c - ii))
    diff = (lax.broadcasted_iota(jnp.int32, (c, c), 0)
            - lax.broadcasted_iota(jnp.int32, (c, c), 1)).astype(F32)
    lgf_c = jnp.concatenate([lgf] * (c // d), axis=1)
    lgb_c = jnp.concatenate([lgb] * (c // d), axis=1)
    dmat = jnp.where(diff >= 0.0, jnp.exp(lgf_c * jnp.maximum(diff, 0.0)),
                     jnp.exp(lgb_c * jnp.maximum(-diff, 0.0)))
    chunk_f = jnp.exp(lgf * float(c))
    chunk_b = jnp.exp(lgb * float(c))

    def kv_body(n, carry):
        rows = pl.ds(pl.multiple_of(n * c, c), c)
        kc = k_ref[0, rows, :].astype(F32)
        kd = jnp.concatenate([kc * kdf, kc * kdb], axis=1).astype(BF16)
        kv_ref[n] = lax.dot_general(kd, v_ref[0, rows, :], (((0,), (0,)), ((), ())),
                                    preferred_element_type=F32)
        return carry

    lax.fori_loop(0, nc, kv_body, 0, unroll=2)

    def fwd_body(n, state):
        r_ref[n, 0:d, :] = state.astype(BF16)
        return chunk_f * state + kv_ref[n, 0:d, :]

    lax.fori_loop(0, nc, fwd_body, jnp.zeros((d, d), F32))

    def bwd_body(m, state):
        n = nc - 1 - m
        r_ref[n, d:2 * d, :] = state.astype(BF16)
        return chunk_b * state + kv_ref[n, d:2 * d, :]

    lax.fori_loop(0, nc, bwd_body, jnp.zeros((d, d), F32))

    def out_body(t, carry):
        staged = []
        for n in (2 * t, 2 * t + 1):
            rows = pl.ds(pl.multiple_of(n * c, c), c)
            qb = q_ref[0, rows, :]
            s = lax.dot_general(qb, k_ref[0, rows, :], (((1,), (1,)), ((), ())),
                                preferred_element_type=F32)
            qc = qb.astype(F32)
            qd = jnp.concatenate([qc * qdf, qc * qdb], axis=1).astype(BF16)
            staged.append((rows, s, jnp.dot(qd, r_ref[n], preferred_element_type=F32)))
        for rows, s, cross in staged:
            o = jnp.dot((s * dmat).astype(BF16), v_ref[0, rows, :], preferred_element_type=F32) + cross
            mu = jnp.mean(o, axis=-1, keepdims=True)
            var = jnp.mean(jnp.square(o - mu), axis=-1, keepdims=True)
            on = (o - mu) * lax.rsqrt(var + EPS)
            g = g_ref[0, rows, :].astype(F32)
            o_ref[0, rows, :] = (on * (g * jax.nn.sigmoid(g))).astype(BF16)
        return carry

    lax.fori_loop(0, nc // 2, out_body, 0)


def _ret_branch(ret, layer, decay_f, decay_b, batch, seq):
    d = RET_HEAD_DIM
    nc = seq // RET_CHUNK
    r3 = ret.reshape(batch, seq, 4 * BRANCH_WIDTH)

    def sec(k):
        return pl.BlockSpec((1, seq, d), lambda b, h: (b, 0, k * RET_HEADS + h))

    decay_spec = pl.BlockSpec((1, 1, 1, d), lambda b, h: (layer, h, 0, 0))
    out = pl.pallas_call(
        functools.partial(_ret_kernel, seq=seq),
        out_shape=jax.ShapeDtypeStruct((batch, seq, BRANCH_WIDTH), BF16),
        grid=(batch, RET_HEADS),
        in_specs=[sec(0), sec(1), sec(2), sec(3), decay_spec, decay_spec],
        out_specs=pl.BlockSpec((1, seq, d), lambda b, h: (b, 0, h)),
        scratch_shapes=[pltpu.VMEM((nc, 2 * d, d), F32), pltpu.VMEM((nc, 2 * d, d), BF16)],
        compiler_params=_params(("arbitrary", "arbitrary")),
        name="retention",
    )(r3, r3, r3, r3, decay_f, decay_b)
    return out.reshape(batch * seq, BRANCH_WIDTH)


def _na_kernel(q_ref, k0_ref, k1_ref, k2_ref, v0_ref, v1_ref, v2_ref, bias_ref, o_ref,
               kw_ref, vw_ref, *, grid_rows):
    rr = NA_ROWS_PER_STEP
    gw = GRID_W
    blk = rr * gw
    kr = NA_ROWS_MAX
    m = pl.program_id(1)
    for idx, (kref, vref) in enumerate(((k0_ref, v0_ref), (k1_ref, v1_ref), (k2_ref, v2_ref))):
        kw_ref[idx * blk:(idx + 1) * blk, :] = kref[0]
        vw_ref[idx * blk:(idx + 1) * blk, :] = vref[0]
    lane = lax.broadcasted_iota(jnp.int32, (gw, 2 * NA_HEAD_DIM), 1)
    low = lane < NA_HEAD_DIM

    pair_cols = [slice(jp * 2 * NA_HEAD_DIM, (jp + 1) * 2 * NA_HEAD_DIM) for jp in range(NA_HEADS // 2)]

    def window(r):
        r_glob = m * rr + r
        rs = jnp.clip(r_glob - kr // 2, 0, grid_rows - kr)
        krows = pl.ds(pl.multiple_of((rs - (m - 1) * rr) * gw, gw), kr * gw)
        qrows = pl.ds(pl.multiple_of(r * gw, gw), gw)
        return qrows, krows, r_glob - rs

    def scores(r):
        qrows, krows, didx = window(r)
        out = []
        for jp, cols in enumerate(pair_cols):
            q2 = q_ref[0, qrows, cols]
            zero = jnp.zeros_like(q2)
            qq = jnp.concatenate([jnp.where(low, q2, zero), jnp.where(low, zero, q2)], axis=0)
            s = lax.dot_general(qq, kw_ref[krows, cols], (((1,), (1,)), ((), ())),
                                preferred_element_type=F32)
            out.append(s + bias_ref[0, didx, jp])
        return out

    def softmax(ss):
        out = []
        for s in ss:
            p = jnp.exp(s - jnp.max(s, axis=-1, keepdims=True))
            out.append((p.astype(BF16), jnp.sum(p, axis=-1, keepdims=True)))
        return out

    def values(r, ps):
        qrows, krows, _ = window(r)
        for (p, l), cols in zip(ps, pair_cols):
            o = jnp.dot(p, vw_ref[krows, cols], preferred_element_type=F32) / l
            o_ref[0, qrows, cols] = jnp.where(low, o[:gw], o[gw:]).astype(BF16)

    def group_body(t, carry):
        rows = [NA_ROW_GROUP * t + u for u in range(NA_ROW_GROUP)]
        staged = [scores(r) for r in rows]
        for r, ss in zip(rows, staged):
            values(r, softmax(ss))
        return carry

    lax.fori_loop(0, rr // NA_ROW_GROUP, group_body, 0)


def _na_bias(rpb):
    rpb = rpb.astype(F32)
    per_q = []
    for qc in range(GRID_W):
        cs = min(max(qc - NA_COLS // 2, 0), GRID_W - NA_COLS)
        lo = (NA_COLS - 1) - qc + cs
        seg = rpb[..., lo:lo + NA_COLS]
        per_q.append(jnp.pad(seg, ((0, 0),) * 3 + ((cs, GRID_W - NA_COLS - cs),), constant_values=NEG_INF))
    toep = jnp.stack(per_q, axis=3)
    per_dl = []
    for dl in range(NA_ROWS_MAX):
        lo = NA_ROWS_MAX - 1 - dl
        win = toep[:, :, lo:lo + NA_ROWS_MAX]
        per_dl.append(jnp.transpose(win, (0, 1, 3, 2, 4)).reshape(
            rpb.shape[0], NA_HEADS, GRID_W, NA_ROWS_MAX * GRID_W))
    table = jnp.stack(per_dl, axis=1)
    return table.reshape(rpb.shape[0], NA_ROWS_MAX, NA_HEADS // 2, 2 * GRID_W, NA_ROWS_MAX * GRID_W)


def _na_branch(na, layer, bias, batch, seq):
    w = BRANCH_WIDTH
    grid_rows = seq // GRID_W
    rr = NA_ROWS_PER_STEP
    blk = rr * GRID_W
    nblk = grid_rows // rr
    n3 = na.reshape(batch, seq, 3 * w)

    def kv_spec(col, shift):
        return pl.BlockSpec((1, blk, w), lambda b, m: (b, jnp.clip(m + shift, 0, nblk - 1), col))

    out = pl.pallas_call(
        functools.partial(_na_kernel, grid_rows=grid_rows),
        out_shape=jax.ShapeDtypeStruct((batch, seq, w), BF16),
        grid=(batch, nblk),
        in_specs=[pl.BlockSpec((1, blk, w), lambda b, m: (b, m, 0)),
                  kv_spec(1, -1), kv_spec(1, 0), kv_spec(1, 1),
                  kv_spec(2, -1), kv_spec(2, 0), kv_spec(2, 1),
                  pl.BlockSpec((1,) + bias.shape[1:], lambda b, m: (layer, 0, 0, 0, 0))],
        out_specs=pl.BlockSpec((1, blk, w), lambda b, m: (b, m, 0)),
        scratch_shapes=[pltpu.VMEM((3 * blk, w), BF16), pltpu.VMEM((3 * blk, w), BF16)],
        compiler_params=_params(("arbitrary", "arbitrary")),
        name="nbr_attn",
    )(n3, n3, n3, n3, n3, n3, n3, bias)
    return out.reshape(batch * seq, w)


def _merge_kernel(h_ref, gate_ref, xs_ref,
                  br0, br1, br2, br3, wg0, wg1, wg2, wg3, bg0, bg1, bg2, bg3,
                  wb0, wb1, wb2, wb3, wo_ref, o_ref, m_ref, *, nj):
    j = pl.program_id(1)

    @pl.when(j < nj)
    def _():
        acc = None
        for br, wg, bg, wb in ((br0, wg0, bg0, wb0), (br1, wg1, bg1, wb1),
                               (br2, wg2, bg2, wb2), (br3, wg3, bg3, wb3)):
            gate = jax.nn.sigmoid(jnp.dot(h_ref[...], wg[0], preferred_element_type=F32) + bg[0])
            term = gate * jnp.dot(br[...].astype(BF16), wb[0, 0], preferred_element_type=F32)
            acc = term if acc is None else acc + term
        m_ref[j] = acc.astype(BF16)

    @pl.when(j >= nj)
    def _():
        tn = wo_ref.shape[3]
        out = jnp.dot(m_ref[0], wo_ref[0, 0, 0:tn, :], preferred_element_type=F32)
        for kk in range(1, nj):
            out = out + jnp.dot(m_ref[kk], wo_ref[0, 0, kk * tn:(kk + 1) * tn, :], preferred_element_type=F32)
        o_ref[...] = xs_ref[...] + gate_ref[0] * out


def _merge(x2, h, layer, mods, branches, w_gate_t, b_gate, w_branch_t, w_out_t, seq):
    rows, d = x2.shape
    tn = TN_MERGE
    nj = d // tn
    tps = seq // TM

    def first(j):
        return jnp.minimum(j, nj - 1)

    def second(j):
        return jnp.maximum(j - nj, 0)

    def gate_w_spec(b):
        return pl.BlockSpec((1, d, tn), lambda i, j: (layer, 0, b * nj + first(j)))

    def gate_b_spec(b):
        return pl.BlockSpec((1, 1, tn), lambda i, j: (layer, 0, b * nj + first(j)))

    def branch_w_spec(b):
        return pl.BlockSpec((1, 1, BRANCH_WIDTH, tn), lambda i, j: (layer, b, 0, first(j)))

    br_spec = pl.BlockSpec((TM, BRANCH_WIDTH), lambda i, j: (i, 0))
    in_specs = [
        pl.BlockSpec((TM, d), lambda i, j: (i, 0)),
        pl.BlockSpec((1, 1, tn), lambda i, j: ((i // tps) * 6 + 2, 0, second(j))),
        pl.BlockSpec((TM, tn), lambda i, j: (i, second(j))),
        br_spec, br_spec, br_spec, br_spec,
        gate_w_spec(0), gate_w_spec(1), gate_w_spec(2), gate_w_spec(3),
        gate_b_spec(0), gate_b_spec(1), gate_b_spec(2), gate_b_spec(3),
        branch_w_spec(0), branch_w_spec(1), branch_w_spec(2), branch_w_spec(3),
        pl.BlockSpec((1, 1, d, tn), lambda i, j: (layer, second(j), 0, 0)),
    ]
    return pl.pallas_call(
        functools.partial(_merge_kernel, nj=nj),
        out_shape=jax.ShapeDtypeStruct((rows, d), F32),
        grid=(rows // TM, 2 * nj),
        in_specs=in_specs,
        out_specs=pl.BlockSpec((TM, tn), lambda i, j: (i, second(j))),
        scratch_shapes=[pltpu.VMEM((nj, TM, tn), BF16)],
        compiler_params=_params(("arbitrary", "arbitrary")),
        name="merge",
    )(h, mods, x2, *branches,
      w_gate_t, w_gate_t, w_gate_t, w_gate_t, b_gate, b_gate, b_gate, b_gate,
      w_branch_t, w_branch_t, w_branch_t, w_branch_t, w_out_t)


def _ffn_kernel(x_ref, xp_ref, xn_ref, g_ref, sc_ref, sh_ref, gate_ref, xs_ref,
                wa_ref, wb_ref, ca_ref, cb_ref, wd_ref,
                o_ref, h_ref, ua0_ref, ub0_ref, ua1_ref, ub1_ref, act_ref, *, tps, nf):
    i = pl.program_id(0)
    j = pl.program_id(1)
    tm = x_ref.shape[0]
    tf = ua0_ref.shape[1]
    ti = i % tps

    ch = FFN_CHUNK
    ext = ch + 2 * 8

    bufs = ((ua0_ref, ub0_ref), (ua1_ref, ub1_ref))

    def up(slot):
        h = h_ref[...]
        bufs[slot][0][...] = jnp.dot(h, wa_ref[0], preferred_element_type=F32)
        bufs[slot][1][...] = jnp.dot(h, wb_ref[0], preferred_element_type=F32)

    def conv(u_ref, c_ref, r0):
        u = u_ref[r0 + HALO - 8:r0 + HALO - 8 + ext, :]
        prev = pltpu.roll(u, 1, 0)[8:8 + ch]
        nxt = pltpu.roll(u, ext - 1, 0)[8:8 + ch]
        return (prev * c_ref[0, 0:1, :] + u[8:8 + ch] * c_ref[0, 1:2, :]
                + nxt * c_ref[0, 2:3, :] + c_ref[0, 3:4, :])

    def activate(slot, tile):
        for r0 in range(0, tm, ch):
            a = conv(bufs[slot][0], ca_ref, r0)
            b = conv(bufs[slot][1], cb_ref, r0)
            act_ref[tile, r0:r0 + ch, :] = (jax.nn.gelu(a, approximate=True) * b).astype(BF16)

    def down():
        out = jnp.dot(act_ref[0], wd_ref[0, 0, 0:tf, :], preferred_element_type=F32)
        for kk in range(1, nf):
            out = out + jnp.dot(act_ref[kk], wd_ref[0, 0, kk * tf:(kk + 1) * tf, :],
                                preferred_element_type=F32)
        o_ref[...] = xs_ref[...] + gate_ref[0] * out

    @pl.when(j == 0)
    def _():
        g = g_ref[0]
        sc = sc_ref[0]
        sh = sh_ref[0]
        _norm_rows(xp_ref, h_ref, 0, g, sc, sh, keep=ti != 0)
        _norm_rows(x_ref, h_ref, HALO, g, sc, sh)
        _norm_rows(xn_ref, h_ref, HALO + tm, g, sc, sh, keep=ti != tps - 1)
        up(0)

    for parity in range(2):
        @pl.when((j >= 1) & (j < nf) & (j % 2 == parity))
        def _():
            activate(1 - parity, j - 1)
            up(parity)

    @pl.when(j == nf)
    def _():
        activate((nf - 1) % 2, nf - 1)
        down()

    @pl.when(j > nf)
    def _():
        down()


def _ffn(x2, layer, g, mods, w_up_t, conv_p, w_down_t, seq):
    rows, d = x2.shape
    tm = TM_FFN
    tps = seq // tm
    nf = D_FF // TF
    tn = TN_DOWN
    nd = d // tn
    hb = tm // HALO
    nhalo = rows // HALO

    def first(j):
        return jnp.minimum(j, nf - 1)

    def second(j):
        return jnp.maximum(j - nf, 0)

    def lagged(j):
        return jnp.clip(j - 1, 0, nf - 1)

    return pl.pallas_call(
        functools.partial(_ffn_kernel, tps=tps, nf=nf),
        out_shape=jax.ShapeDtypeStruct((rows, d), F32),
        grid=(rows // tm, nf + nd),
        in_specs=[
            pl.BlockSpec((tm, d), lambda i, j: (i, 0), pipeline_mode=pl.Buffered(1)),
            pl.BlockSpec((HALO, d), lambda i, j: (jnp.maximum(i * hb - 1, 0), 0)),
            pl.BlockSpec((HALO, d), lambda i, j: (jnp.minimum((i + 1) * hb, nhalo - 1), 0)),
            pl.BlockSpec((1, 1, d), lambda i, j: (layer, 0, 0)),
            pl.BlockSpec((1, 1, d), lambda i, j: ((i // tps) * 6 + 4, 0, 0)),
            pl.BlockSpec((1, 1, d), lambda i, j: ((i // tps) * 6 + 3, 0, 0)),
            pl.BlockSpec((1, 1, tn), lambda i, j: ((i // tps) * 6 + 5, 0, second(j))),
            pl.BlockSpec((tm, tn), lambda i, j: (i, second(j))),
            pl.BlockSpec((1, d, TF), lambda i, j: (layer, 0, first(j))),
            pl.BlockSpec((1, d, TF), lambda i, j: (layer, 0, nf + first(j))),
            pl.BlockSpec((1, 8, TF), lambda i, j: (layer, 0, lagged(j))),
            pl.BlockSpec((1, 8, TF), lambda i, j: (layer, 0, nf + lagged(j))),
            pl.BlockSpec((1, 1, D_FF, tn), lambda i, j: (layer, second(j), 0, 0)),
        ],
        out_specs=pl.BlockSpec((tm, tn), lambda i, j: (i, second(j))),
        scratch_shapes=[pltpu.VMEM((tm + 2 * HALO, d), BF16),
                        pltpu.VMEM((tm + 2 * HALO, TF), F32),
                        pltpu.VMEM((tm + 2 * HALO, TF), F32),
                        pltpu.VMEM((tm + 2 * HALO, TF), F32),
                        pltpu.VMEM((tm + 2 * HALO, TF), F32),
                        pltpu.VMEM((nf, tm, TF), BF16)],
        compiler_params=_params(("arbitrary", "arbitrary")),
        name="conv_ffn",
    )(x2, x2, x2, g, mods, mods, mods, x2, w_up_t, w_up_t, conv_p, conv_p, w_down_t)


def _final_kernel(x_ref, g_ref, o_ref):
    x = x_ref[...]
    ms = jnp.mean(x * x, axis=-1, keepdims=True)
    o_ref[...] = x * lax.rsqrt(ms + EPS) * g_ref[...]


def _final_norm(x2, g_final):
    rows, d = x2.shape
    tr = 256
    return pl.pallas_call(
        _final_kernel,
        out_shape=jax.ShapeDtypeStruct((rows, d), F32),
        grid=(rows // tr,),
        in_specs=[pl.BlockSpec((tr, d), lambda i: (i, 0)), pl.BlockSpec((1, d), lambda i: (0, 0))],
        out_specs=pl.BlockSpec((tr, d), lambda i: (i, 0)),
        compiler_params=_params(("arbitrary",)),
        name="final_norm",
    )(x2, g_final)


def _rope_tables(seq):
    half = RET_HEAD_DIM // 2
    inv = ROPE_BASE ** (-jnp.arange(half, dtype=F32) / half)
    ang = jnp.arange(seq, dtype=F32)[:, None] * inv[None, :]
    cos = jnp.cos(ang)
    sin = jnp.sin(ang)
    return jnp.concatenate([cos, cos], axis=1), jnp.concatenate([-sin, sin], axis=1)


def _trunk(x, mods, wts):
    batch, seq, d = x.shape
    x2 = x.reshape(batch * seq, d)
    cos_t, sin_t = _rope_tables(seq)
    fc = _fourier_consts(seq)
    for l in range(DEPTH):
        m = mods[l]
        pool_in, four_in, ret_in, na_in, h = _in_proj(x2, l, wts["g_mix"], m, cos_t, sin_t, wts["w_in"], seq)
        branches = (
            _pool_branch(pool_in, l, wts["pool_w"], wts["pool_scale"], seq),
            _fourier_branch(four_in, fc, batch, seq),
            _ret_branch(ret_in, l, wts["ret_decay_f"], wts["ret_decay_b"], batch, seq),
            _na_branch(na_in, l, wts["na_bias"], batch, seq),
        )
        x2 = _merge(x2, h, l, m, branches, wts["w_gate"], wts["b_gate"], wts["w_branch"], wts["w_out"], seq)
        x2 = _ffn(x2, l, wts["g_ffn"], m, wts["w_up"], wts["conv_p"], wts["w_down"], seq)
    return _final_norm(x2, wts["g_final"]).reshape(batch, seq, d)


def kernel(x_prompt, x_sample, c_prompt, c_sample, w_ada, b_ada, g_mix, w_in, pool_w, pool_scale,
           ret_decay_f, ret_decay_b, na_rpb, w_branch, w_gate, b_gate, w_out, g_ffn, w_up, conv_w,
           conv_b, w_down, g_final):
    d = D_MODEL
    nb_p = c_prompt.shape[0]
    nb_s = c_sample.shape[0]
    c8 = jnp.concatenate([c_prompt, c_sample, jnp.zeros((8 - nb_p - nb_s, d), F32)], axis=0)
    mod_all = _ada(c8, w_ada, b_ada)
    mods_p = mod_all[:, :nb_p].reshape(DEPTH, nb_p * 6, 1, d)
    mods_s = mod_all[:, nb_p:nb_p + nb_s].reshape(DEPTH, nb_s * 6, 1, d)
    decay_shape = (DEPTH, RET_HEADS, 1, RET_HEAD_DIM)
    wts = dict(
        g_mix=g_mix.reshape(DEPTH, 1, d),
        w_in=w_in.astype(BF16),
        pool_w=pool_w.astype(BF16),
        pool_scale=pool_scale.reshape(DEPTH, 1, BRANCH_WIDTH),
        ret_decay_f=jnp.broadcast_to(ret_decay_f.astype(F32)[:, :, None, None], decay_shape),
        ret_decay_b=jnp.broadcast_to(ret_decay_b.astype(F32)[:, :, None, None], decay_shape),
        na_bias=_na_bias(na_rpb),
        w_branch=w_branch.astype(BF16),
        w_gate=w_gate.astype(BF16),
        b_gate=b_gate.reshape(DEPTH, 1, N_BRANCH * d),
        w_out=_col_tiles(w_out, TN_MERGE),
        g_ffn=g_ffn.reshape(DEPTH, 1, d),
        w_up=w_up.astype(BF16),
        conv_p=jnp.concatenate([conv_w, conv_b[:, None, :], jnp.zeros((DEPTH, 4, 2 * D_FF), F32)], axis=1),
        w_down=_col_tiles(w_down, TN_DOWN),
        g_final=g_final.reshape(1, d),
    )
    y_prompt = _trunk(x_prompt, mods_p, wts)
    y_sample = _trunk(x_sample, mods_s, wts)
    return (y_prompt, y_sample)
```

```python
import functools
import math

import numpy as np
import jax
import jax.numpy as jnp
from jax import lax
from jax.experimental import pallas as pl
from jax.experimental.pallas import tpu as pltpu

F32 = jnp.float32
BF16 = jnp.bfloat16

D_MODEL = 2048
DEPTH = 4
GRID_W = 64
BRANCH_WIDTH = 512
N_BRANCH = 4
POOL_WINDOWS = (2, 4, 8, 16)
POOL_GROUP = 128
FOURIER_GROUP = 128
RET_HEADS = 4
RET_HEAD_DIM = 128
ROPE_BASE = 10000.0
NA_HEADS = 8
NA_HEAD_DIM = 64
NA_ROWS_MAX = 8
NA_COLS = 16
D_FF = 5632
EPS = 1e-6
NEG_INF = -1e30
IN_WIDTH = 9 * BRANCH_WIDTH

VMEM_LIMIT_BYTES = 56 * 1024 * 1024
BF16_SUBLANES = 16

TM = 1024
TN_IN = 512
TN_MERGE = 256
TM_FFN = 1024
TF = 512
TN_DOWN = 256
HALO = BF16_SUBLANES
NORM_CHUNK = 32
NORM_UNROLL = 4
FFN_CHUNK = 32
POOL_TILE = 512
POOL_HALO = 8
F_T2 = 128
F1_TILE = 8
F2_TILE = 8
RET_CHUNK = 256
NA_ROWS_PER_STEP = 8
NA_ROW_GROUP = 4


def _params(sem):
    return pltpu.CompilerParams(dimension_semantics=sem, vmem_limit_bytes=VMEM_LIMIT_BYTES)


def _norm_rows(x_ref, dst_ref, dst_off, g, scale, shift, keep=None):
    rows = x_ref.shape[0]
    ch = min(rows, NORM_CHUNK)
    gain = g * (1.0 + scale)

    def body(c, carry):
        r0 = pl.multiple_of(c * ch, ch)
        x = x_ref[pl.ds(r0, ch), :]
        ms = jnp.mean(x * x, axis=-1, keepdims=True)
        h = (x * lax.rsqrt(ms + EPS)) * gain + shift
        if keep is not None:
            h = jnp.where(keep, h, 0.0)
        dst_ref[pl.ds(dst_off + r0, ch), :] = h.astype(BF16)
        return carry

    lax.fori_loop(0, rows // ch, body, 0, unroll=min(NORM_UNROLL, rows // ch))


def _ada_kernel(c_ref, w_ref, b_ref, o_ref):
    c = c_ref[...]
    s = (c * jax.nn.sigmoid(c)).astype(BF16)
    o_ref[0] = jnp.dot(s, w_ref[0].astype(BF16), preferred_element_type=F32) + b_ref[0]


def _ada(c8, w_ada, b_ada):
    depth, d, n = w_ada.shape
    tn = 1024
    return pl.pallas_call(
        _ada_kernel,
        out_shape=jax.ShapeDtypeStruct((depth, 8, n), F32),
        grid=(depth, n // tn),
        in_specs=[
            pl.BlockSpec((8, d), lambda l, j: (0, 0)),
            pl.BlockSpec((1, d, tn), lambda l, j: (l, 0, j)),
            pl.BlockSpec((1, 1, tn), lambda l, j: (l, 0, j)),
        ],
        out_specs=pl.BlockSpec((1, 8, tn), lambda l, j: (l, 0, j)),
        compiler_params=_params(("arbitrary", "arbitrary")),
        name="ada",
    )(c8, w_ada, b_ada.reshape(depth, 1, n))


def _rope_tile(acc, cos, sin, scale):
    parts = []
    for hh in range(acc.shape[1] // RET_HEAD_DIM):
        ch = acc[:, hh * RET_HEAD_DIM:(hh + 1) * RET_HEAD_DIM]
        parts.append((ch * cos + pltpu.roll(ch, RET_HEAD_DIM // 2, 1) * sin) * scale)
    return jnp.concatenate(parts, axis=1)


def _in_kernel(xa_ref, xb_ref, g_ref, sc_ref, sh_ref, cos_ref, sin_ref, w_ref,
               pool_ref, four_ref, ret_ref, na_ref, h_ref):
    j = pl.program_id(1)

    @pl.when(j == 0)
    def _():
        _norm_rows(xa_ref, h_ref, 0, g_ref[0], sc_ref[0], sh_ref[0])
        _norm_rows(xb_ref, h_ref, xa_ref.shape[0], g_ref[0], sc_ref[0], sh_ref[0])

    def proj():
        return jnp.dot(h_ref[...], w_ref[0], preferred_element_type=F32)

    @pl.when(j == 0)
    def _():
        pool_ref[...] = proj()

    @pl.when(j == 1)
    def _():
        four_ref[...] = proj()

    @pl.when(j == 2)
    def _():
        ret_ref[...] = _rope_tile(proj(), cos_ref[...], sin_ref[...], 1.0).astype(BF16)

    @pl.when(j == 3)
    def _():
        ret_ref[...] = _rope_tile(proj(), cos_ref[...], sin_ref[...], RET_HEAD_DIM ** -0.5).astype(BF16)

    @pl.when((j == 4) | (j == 5))
    def _():
        ret_ref[...] = proj().astype(BF16)

    @pl.when(j == 6)
    def _():
        na_ref[...] = (proj() * (NA_HEAD_DIM ** -0.5)).astype(BF16)

    @pl.when(j >= 7)
    def _():
        na_ref[...] = proj().astype(BF16)


def _in_proj(x2, layer, g, mods, cos_t, sin_t, w_in_t, seq):
    rows, d = x2.shape
    tps = seq // TM
    nj = IN_WIDTH // TN_IN
    last_half = 2 * (rows // TM)

    def mod_spec(k):
        return pl.BlockSpec((1, 1, d), lambda i, j: ((i // tps) * 6 + k, 0, 0))

    return pl.pallas_call(
        _in_kernel,
        out_shape=(
            jax.ShapeDtypeStruct((rows, BRANCH_WIDTH), F32),
            jax.ShapeDtypeStruct((rows, BRANCH_WIDTH), F32),
            jax.ShapeDtypeStruct((rows, 4 * BRANCH_WIDTH), BF16),
            jax.ShapeDtypeStruct((rows, 3 * BRANCH_WIDTH), BF16),
            jax.ShapeDtypeStruct((rows, d), BF16),
        ),
        grid=(rows // TM, nj),
        in_specs=[
            pl.BlockSpec((TM // 2, d), lambda i, j: (jnp.minimum(2 * (i + j // (nj - 1)), last_half - 2), 0)),
            pl.BlockSpec((TM // 2, d), lambda i, j: (2 * i + 1, 0)),
            pl.BlockSpec((1, 1, d), lambda i, j: (layer, 0, 0)),
            mod_spec(1),
            mod_spec(0),
            pl.BlockSpec((TM, RET_HEAD_DIM), lambda i, j: (i % tps, 0)),
            pl.BlockSpec((TM, RET_HEAD_DIM), lambda i, j: (i % tps, 0)),
            pl.BlockSpec((1, d, TN_IN), lambda i, j: (layer, 0, j)),
        ],
        out_specs=(
            pl.BlockSpec((TM, TN_IN), lambda i, j: (i, 0)),
            pl.BlockSpec((TM, TN_IN), lambda i, j: (i, 0)),
            pl.BlockSpec((TM, TN_IN), lambda i, j: (i, jnp.clip(j - 2, 0, 3))),
            pl.BlockSpec((TM, TN_IN), lambda i, j: (i, jnp.clip(j - 6, 0, 2))),
            pl.BlockSpec((TM, d), lambda i, j: (i, 0)),
        ),
        compiler_params=_params(("arbitrary", "arbitrary")),
        name="in_proj",
    )(x2, x2, g, mods, mods, cos_t, sin_t, w_in_t)


def _pool_kernel(a_ref, ap_ref, an_ref, w_ref, s_ref, o_ref, ext_ref, *, tps, seq):
    tp = a_ref.shape[0]
    ti = pl.program_id(0) % tps
    ext_ref[0:POOL_HALO, :] = jnp.where(ti == 0, 0.0, ap_ref[...])
    ext_ref[POOL_HALO:POOL_HALO + tp, :] = a_ref[...]
    ext_ref[POOL_HALO + tp:2 * POOL_HALO + tp, :] = jnp.where(ti == tps - 1, 0.0, an_ref[...])
    t = ti * tp + lax.broadcasted_iota(jnp.int32, (tp, 1), 0)
    for gi, w in enumerate(POOL_WINDOWS):
        cols = slice(gi * POOL_GROUP, (gi + 1) * POOL_GROUP)
        half = w // 2
        acc = ext_ref[pl.ds(POOL_HALO - half, tp), cols]
        for k in range(-half + 1, half):
            acc = acc + ext_ref[pl.ds(POOL_HALO + k, tp), cols]
        count = (jnp.minimum(t + half, seq) - jnp.maximum(t - half, 0)).astype(F32)
        pooled = acc / count - a_ref[:, cols]
        y = jnp.dot(pooled.astype(BF16), w_ref[0, gi], preferred_element_type=F32) * s_ref[0, :, cols]
        o_ref[:, cols] = y.astype(BF16)


def _pool_branch(a, layer, pool_w, pool_scale, seq):
    rows, w = a.shape
    tp = POOL_TILE
    tps = seq // tp
    hb = tp // POOL_HALO
    nhalo = rows // POOL_HALO
    return pl.pallas_call(
        functools.partial(_pool_kernel, tps=tps, seq=seq),
        out_shape=jax.ShapeDtypeStruct((rows, w), BF16),
        grid=(rows // tp,),
        in_specs=[
            pl.BlockSpec((tp, w), lambda i: (i, 0)),
            pl.BlockSpec((POOL_HALO, w), lambda i: (jnp.maximum(i * hb - 1, 0), 0)),
            pl.BlockSpec((POOL_HALO, w), lambda i: (jnp.minimum((i + 1) * hb, nhalo - 1), 0)),
            pl.BlockSpec((1, len(POOL_WINDOWS), POOL_GROUP, POOL_GROUP), lambda i: (layer, 0, 0, 0)),
            pl.BlockSpec((1, 1, w), lambda i: (layer, 0, 0)),
        ],
        out_specs=pl.BlockSpec((tp, w), lambda i: (i, 0)),
        scratch_shapes=[pltpu.VMEM((tp + 2 * POOL_HALO, w), F32)],
        compiler_params=_params(("arbitrary",)),
        name="pool",
    )(a, a, a, pool_w, pool_scale)


def _f1_kernel(x_ref, k1_ref, tc_ref, ts_ref, yr_ref, yi_ref):
    _, n1, t, w = x_ref.shape
    x = x_ref[0].reshape(n1 * t, w).astype(BF16)
    y = jnp.dot(k1_ref[...], x, preferred_element_type=F32)
    yr = y[:n1 * t]
    yi = y[n1 * t:]
    reps = w // tc_ref.shape[2]
    c = jnp.concatenate([tc_ref[...].reshape(n1 * t, w // reps)] * reps, axis=1)
    s = jnp.concatenate([ts_ref[...].reshape(n1 * t, w // reps)] * reps, axis=1)
    yr_ref[0] = (yr * c + yi * s).reshape(n1, t, w)
    yi_ref[0] = (yi * c - yr * s).reshape(n1, t, w)


def _f2_kernel(yr_ref, yi_ref, ma_ref, mb_ref, cc_ref, cs_ref, o_ref, zr_ref, zi_ref, *, scale):
    n2 = F_T2
    for s in range(F2_TILE):
        rows = slice(s * n2, (s + 1) * n2)
        z = (jnp.dot(ma_ref[...], yr_ref[0, rows, :].astype(BF16), preferred_element_type=F32)
             + jnp.dot(mb_ref[...], yi_ref[0, rows, :].astype(BF16), preferred_element_type=F32))
        zr_ref[rows, :] = z[:n2].astype(BF16)
        zi_ref[rows, :] = z[n2:].astype(BF16)
    o = (jnp.dot(zr_ref[...], cc_ref[...], preferred_element_type=F32)
         + jnp.dot(zi_ref[...], cs_ref[...], preferred_element_type=F32)) * scale
    for s in range(F2_TILE):
        o_ref[0, :, s, :] = o[s * n2:(s + 1) * n2]


def _dft_cos_sin(n):
    k = np.arange(n)
    ang = 2.0 * np.pi * ((k[:, None] * k[None, :]) % n) / n
    return np.cos(ang), np.sin(ang)


def _fourier_consts(seq):
    n2 = F_T2
    n1 = seq // n2
    c1, s1 = _dft_cos_sin(n1)
    c2, s2 = _dft_cos_sin(n2)
    cg, sg = _dft_cos_sin(FOURIER_GROUP)
    eye = np.eye(BRANCH_WIDTH // FOURIER_GROUP)
    k1 = jnp.asarray(np.kron(np.concatenate([c1, -s1], axis=0), np.eye(F1_TILE)), BF16)
    ma = jnp.asarray(np.concatenate([c2, -s2], axis=0), BF16)
    mb = jnp.asarray(np.concatenate([s2, c2], axis=0), BF16)
    cc = jnp.asarray(np.kron(eye, cg), BF16)
    cs = jnp.asarray(np.kron(eye, sg), BF16)
    prod = (jnp.arange(n1, dtype=jnp.int32)[:, None] * jnp.arange(n2, dtype=jnp.int32)[None, :]) % seq
    ang = prod.astype(F32) * (2.0 * math.pi / seq)
    shape = (n1, n2, FOURIER_GROUP)
    tc = jnp.broadcast_to(jnp.cos(ang)[:, :, None], shape)
    ts = jnp.broadcast_to(jnp.sin(ang)[:, :, None], shape)
    return dict(k1=k1, ma=ma, mb=mb, cc=cc, cs=cs, tc=tc, ts=ts)


def _fourier_branch(f, fc, batch, seq):
    w = BRANCH_WIDTH
    n2 = F_T2
    n1 = seq // n2
    x4 = f.reshape(batch, n1, n2, w)
    y_spec = pl.BlockSpec((1, n1, F1_TILE, w), lambda j, b: (b, 0, j, 0))
    tw_spec = pl.BlockSpec((n1, F1_TILE, FOURIER_GROUP), lambda j, b: (0, j, 0))
    yr, yi = pl.pallas_call(
        _f1_kernel,
        out_shape=(jax.ShapeDtypeStruct((batch, n1, n2, w), F32),) * 2,
        grid=(n2 // F1_TILE, batch),
        in_specs=[
            y_spec,
            pl.BlockSpec((2 * n1 * F1_TILE, n1 * F1_TILE), lambda j, b: (0, 0)),
            tw_spec,
            tw_spec,
        ],
        out_specs=(y_spec, y_spec),
        compiler_params=_params(("arbitrary", "arbitrary")),
        name="fourier1",
    )(x4, fc["k1"], fc["tc"], fc["ts"])
    yr = yr.reshape(batch, n1 * n2, w)
    yi = yi.reshape(batch, n1 * n2, w)
    rows = F2_TILE * n2
    out = pl.pallas_call(
        functools.partial(_f2_kernel, scale=1.0 / math.sqrt(seq * FOURIER_GROUP)),
        out_shape=jax.ShapeDtypeStruct((batch, n2, n1, w), F32),
        grid=(batch, n1 // F2_TILE),
        in_specs=[
            pl.BlockSpec((1, rows, w), lambda b, j: (b, j, 0)),
            pl.BlockSpec((1, rows, w), lambda b, j: (b, j, 0)),
            pl.BlockSpec((2 * n2, n2), lambda b, j: (0, 0)),
            pl.BlockSpec((2 * n2, n2), lambda b, j: (0, 0)),
            pl.BlockSpec((w, w), lambda b, j: (0, 0)),
            pl.BlockSpec((w, w), lambda b, j: (0, 0)),
        ],
        out_specs=pl.BlockSpec((1, n2, F2_TILE, w), lambda b, j: (b, 0, j, 0)),
        scratch_shapes=[pltpu.VMEM((rows, w), BF16), pltpu.VMEM((rows, w), BF16)],
        compiler_params=_params(("arbitrary", "arbitrary")),
        name="fourier2",
    )(yr, yi, fc["ma"], fc["mb"], fc["cc"], fc["cs"])
    return out.reshape(batch * seq, w)


def _log_sigmoid(x):
    return jnp.minimum(x, 0.0) - jnp.log1p(jnp.exp(-jnp.abs(x)))


def _ret_kernel(q_ref, k_ref, v_ref, g_ref, df_ref, db_ref, o_ref, kv_ref, r_ref, *, seq):
    c = RET_CHUNK
    d = RET_HEAD_DIM
    nc = seq // c
    lgf = _log_sigmoid(df_ref[0, 0])
    lgb = _log_sigmoid(db_ref[0, 0])
    ii = lax.broadcasted_iota(jnp.int32, (c, 1), 0).astype(F32)
    kdf = jnp.exp(lgf * (c - 1.0 - ii))
    kdb = jnp.exp(lgb * ii)
    qdf = jnp.exp(lgf * (ii + 1.0))
    qdb = jnp.exp(lgb * (c - ii))
    diff = (lax.broadcasted_iota(jnp.int32, (c, c), 0)
            - lax.broadcasted_iota(jnp.int32, (c, c), 1)).astype(F32)
    lgf_c = jnp.concatenate([lgf] * (c // d), axis=1)
    lgb_c = jnp.concatenate([lgb] * (c // d), axis=1)
    dmat = jnp.where(diff >= 0.0, jnp.exp(lgf_c * jnp.maximum(diff, 0.0)),
                     jnp.exp(lgb_c * jnp.maximum(-diff, 0.0)))
    chunk_f = jnp.exp(lgf * float(c))
    chunk_b = jnp.exp(lgb * float(c))

    def kv_body(n, carry):
        rows = pl.ds(pl.multiple_of(n * c, c), c)
        kc = k_ref[0, rows, :].astype(F32)
        kd = jnp.concatenate([kc * kdf, kc * kdb], axis=1).astype(BF16)
        kv_ref[n] = lax.dot_general(kd, v_ref[0, rows, :], (((0,), (0,)), ((), ())),
                                    preferred_element_type=F32)
        return carry

    lax.fori_loop(0, nc, kv_body, 0, unroll=2)

    def fwd_body(n, state):
        r_ref[n, 0:d, :] = state.astype(BF16)
        return chunk_f * state + kv_ref[n, 0:d, :]

    lax.fori_loop(0, nc, fwd_body, jnp.zeros((d, d), F32))

    def bwd_body(m, state):
        n = nc - 1 - m
        r_ref[n, d:2 * d, :] = state.astype(BF16)
        return chunk_b * state + kv_ref[n, d:2 * d, :]

    lax.fori_loop(0, nc, bwd_body, jnp.zeros((d, d), F32))

    def out_body(t, carry):
        staged = []
        for n in (2 * t, 2 * t + 1):
            rows = pl.ds(pl.multiple_of(n * c, c), c)
            qb = q_ref[0, rows, :]
            s = lax.dot_general(qb, k_ref[0, rows, :], (((1,), (1,)), ((), ())),
                                preferred_element_type=F32)
            qc = qb.astype(F32)
            qd = jnp.concatenate([qc * qdf, qc * qdb], axis=1).astype(BF16)
            staged.append((rows, s, jnp.dot(qd, r_ref[n], preferred_element_type=F32)))
        for rows, s, cross in staged:
            o = jnp.dot((s * dmat).astype(BF16), v_ref[0, rows, :], preferred_element_type=F32) + cross
            mu = jnp.mean(o, axis=-1, keepdims=True)
            var = jnp.mean(jnp.square(o - mu), axis=-1, keepdims=True)
            on = (o - mu) * lax.rsqrt(var + EPS)
            g = g_ref[0, rows, :].astype(F32)
            o_ref[0, rows, :] = (on * (g * jax.nn.sigmoid(g))).astype(BF16)
        return carry

    lax.fori_loop(0, nc // 2, out_body, 0)


def _ret_branch(ret, layer, decay_f, decay_b, batch, seq):
    d = RET_HEAD_DIM
    nc = seq // RET_CHUNK
    r3 = ret.reshape(batch, seq, 4 * BRANCH_WIDTH)

    def sec(k):
        return pl.BlockSpec((1, seq, d), lambda b, h: (b, 0, k * RET_HEADS + h))

    decay_spec = pl.BlockSpec((1, 1, 1, d), lambda b, h: (layer, h, 0, 0))
    out = pl.pallas_call(
        functools.partial(_ret_kernel, seq=seq),
        out_shape=jax.ShapeDtypeStruct((batch, seq, BRANCH_WIDTH), BF16),
        grid=(batch, RET_HEADS),
        in_specs=[sec(0), sec(1), sec(2), sec(3), decay_spec, decay_spec],
        out_specs=pl.BlockSpec((1, seq, d), lambda b, h: (b, 0, h)),
        scratch_shapes=[pltpu.VMEM((nc, 2 * d, d), F32), pltpu.VMEM((nc, 2 * d, d), BF16)],
        compiler_params=_params(("arbitrary", "arbitrary")),
        name="retention",
    )(r3, r3, r3, r3, decay_f, decay_b)
    return out.reshape(batch * seq, BRANCH_WIDTH)


def _na_kernel(q_ref, k0_ref, k1_ref, k2_ref, v0_ref, v1_ref, v2_ref, bias_ref, o_ref,
               kw_ref, vw_ref, *, grid_rows):
    rr = NA_ROWS_PER_STEP
    gw = GRID_W
    blk = rr * gw
    kr = NA_ROWS_MAX
    m = pl.program_id(1)
    for idx, (kref, vref) in enumerate(((k0_ref, v0_ref), (k1_ref, v1_ref), (k2_ref, v2_ref))):
        kw_ref[idx * blk:(idx + 1) * blk, :] = kref[0]
        vw_ref[idx * blk:(idx + 1) * blk, :] = vref[0]
    lane = lax.broadcasted_iota(jnp.int32, (gw, 2 * NA_HEAD_DIM), 1)
    low = lane < NA_HEAD_DIM

    pair_cols = [slice(jp * 2 * NA_HEAD_DIM, (jp + 1) * 2 * NA_HEAD_DIM) for jp in range(NA_HEADS // 2)]

    def window(r):
        r_glob = m * rr + r
        rs = jnp.clip(r_glob - kr // 2, 0, grid_rows - kr)
        krows = pl.ds(pl.multiple_of((rs - (m - 1) * rr) * gw, gw), kr * gw)
        qrows = pl.ds(pl.multiple_of(r * gw, gw), gw)
        return qrows, krows, r_glob - rs

    def scores(r):
        qrows, krows, didx = window(r)
        out = []
        for jp, cols in enumerate(pair_cols):
            q2 = q_ref[0, qrows, cols]
            zero = jnp.zeros_like(q2)
            qq = jnp.concatenate([jnp.where(low, q2, zero), jnp.where(low, zero, q2)], axis=0)
            s = lax.dot_general(qq, kw_ref[krows, cols], (((1,), (1,)), ((), ())),
                                preferred_element_type=F32)
            out.append(s + bias_ref[0, didx, jp])
        return out

    def softmax(ss):
        out = []
        for s in ss:
            p = jnp.exp(s - jnp.max(s, axis=-1, keepdims=True))
            out.append((p.astype(BF16), jnp.sum(p, axis=-1, keepdims=True)))
        return out

    def values(r, ps):
        qrows, krows, _ = window(r)
        for (p, l), cols in zip(ps, pair_cols):
            o = jnp.dot(p, vw_ref[krows, cols], preferred_element_type=F32) / l
            o_ref[0, qrows, cols] = jnp.where(low, o[:gw], o[gw:]).astype(BF16)

    def group_body(t, carry):
        rows = [NA_ROW_GROUP * t + u for u in range(NA_ROW_GROUP)]
        staged = [scores(r) for r in rows]
        for r, ss in zip(rows, staged):
            values(r, softmax(ss))
        return carry

    lax.fori_loop(0, rr // NA_ROW_GROUP, group_body, 0)


def _na_bias(rpb):
    rpb = rpb.astype(F32)
    per_q = []
    for qc in range(GRID_W):
        cs = min(max(qc - NA_COLS // 2, 0), GRID_W - NA_COLS)
        lo = (NA_COLS - 1) - qc + cs
        seg = rpb[..., lo:lo + NA_COLS]
        per_q.append(jnp.pad(seg, ((0, 0),) * 3 + ((cs, GRID_W - NA_COLS - cs),), constant_values=NEG_INF))
    toep = jnp.stack(per_q, axis=3)
    per_dl = []
    for dl in range(NA_ROWS_MAX):
        lo = NA_ROWS_MAX - 1 - dl
        win = toep[:, :, lo:lo + NA_ROWS_MAX]
        per_dl.append(jnp.transpose(win, (0, 1, 3, 2, 4)).reshape(
            rpb.shape[0], NA_HEADS, GRID_W, NA_ROWS_MAX * GRID_W))
    table = jnp.stack(per_dl, axis=1)
    return table.reshape(rpb.shape[0], NA_ROWS_MAX, NA_HEADS // 2, 2 * GRID_W, NA_ROWS_MAX * GRID_W)


def _na_branch(na, layer, bias, batch, seq):
    w = BRANCH_WIDTH
    grid_rows = seq // GRID_W
    rr = NA_ROWS_PER_STEP
    blk = rr * GRID_W
    nblk = grid_rows // rr
    n3 = na.reshape(batch, seq, 3 * w)

    def kv_spec(col, shift):
        return pl.BlockSpec((1, blk, w), lambda b, m: (b, jnp.clip(m + shift, 0, nblk - 1), col))

    out = pl.pallas_call(
        functools.partial(_na_kernel, grid_rows=grid_rows),
        out_shape=jax.ShapeDtypeStruct((batch, seq, w), BF16),
        grid=(batch, nblk),
        in_specs=[pl.BlockSpec((1, blk, w), lambda b, m: (b, m, 0)),
                  kv_spec(1, -1), kv_spec(1, 0), kv_spec(1, 1),
                  kv_spec(2, -1), kv_spec(2, 0), kv_spec(2, 1),
                  pl.BlockSpec((1,) + bias.shape[1:], lambda b, m: (layer, 0, 0, 0, 0))],
        out_specs=pl.BlockSpec((1, blk, w), lambda b, m: (b, m, 0)),
        scratch_shapes=[pltpu.VMEM((3 * blk, w), BF16), pltpu.VMEM((3 * blk, w), BF16)],
        compiler_params=_params(("arbitrary", "arbitrary")),
        name="nbr_attn",
    )(n3, n3, n3, n3, n3, n3, n3, bias)
    return out.reshape(batch * seq, w)


def _merge_kernel(h_ref, gate_ref, xs_ref,
                  br0, br1, br2, br3, wg0, wg1, wg2, wg3, bg0, bg1, bg2, bg3,
                  wb0, wb1, wb2, wb3, wo_ref, o_ref, m_ref, *, nj):
    j = pl.program_id(1)

    @pl.when(j < nj)
    def _():
        acc = None
        for br, wg, bg, wb in ((br0, wg0, bg0, wb0), (br1, wg1, bg1, wb1),
                               (br2, wg2, bg2, wb2), (br3, wg3, bg3, wb3)):
            gate = jax.nn.sigmoid(jnp.dot(h_ref[...], wg[0], preferred_element_type=F32) + bg[0])
            term = gate * jnp.dot(br[...].astype(BF16), wb[0, 0], preferred_element_type=F32)
            acc = term if acc is None else acc + term
        m_ref[j] = acc.astype(BF16)

    @pl.when(j >= nj)
    def _():
        tn = wo_ref.shape[2]
        out = jnp.dot(m_ref[0], wo_ref[0, 0:tn, :], preferred_element_type=F32)
        for kk in range(1, nj):
            out = out + jnp.dot(m_ref[kk], wo_ref[0, kk * tn:(kk + 1) * tn, :], preferred_element_type=F32)
        o_ref[...] = xs_ref[...] + gate_ref[0] * out


def _merge(x2, h, layer, mods, branches, w_gate_t, b_gate, w_branch_t, w_out_t, seq):
    rows, d = x2.shape
    tn = TN_MERGE
    nj = d // tn
    tps = seq // TM

    def first(j):
        return jnp.minimum(j, nj - 1)

    def second(j):
        return jnp.maximum(j - nj, 0)

    def gate_w_spec(b):
        return pl.BlockSpec((1, d, tn), lambda i, j: (layer, 0, b * nj + first(j)))

    def gate_b_spec(b):
        return pl.BlockSpec((1, 1, tn), lambda i, j: (layer, 0, b * nj + first(j)))

    def branch_w_spec(b):
        return pl.BlockSpec((1, 1, BRANCH_WIDTH, tn), lambda i, j: (layer, b, 0, first(j)))

    def ahead(i, j):
        return jnp.minimum(i + j // nj, rows // TM - 1)

    br_spec = pl.BlockSpec((TM, BRANCH_WIDTH), lambda i, j: (ahead(i, j), 0))
    in_specs = [
        pl.BlockSpec((TM, d), lambda i, j: (ahead(i, j), 0)),
        pl.BlockSpec((1, 1, tn), lambda i, j: ((i // tps) * 6 + 2, 0, second(j))),
        pl.BlockSpec((TM, tn), lambda i, j: (i, second(j))),
        br_spec, br_spec, br_spec, br_spec,
        gate_w_spec(0), gate_w_spec(1), gate_w_spec(2), gate_w_spec(3),
        gate_b_spec(0), gate_b_spec(1), gate_b_spec(2), gate_b_spec(3),
        branch_w_spec(0), branch_w_spec(1), branch_w_spec(2), branch_w_spec(3),
        pl.BlockSpec((1, d, tn), lambda i, j: (layer, 0, second(j))),
    ]
    return pl.pallas_call(
        functools.partial(_merge_kernel, nj=nj),
        out_shape=jax.ShapeDtypeStruct((rows, d), F32),
        grid=(rows // TM, 2 * nj),
        in_specs=in_specs,
        out_specs=pl.BlockSpec((TM, tn), lambda i, j: (i, second(j))),
        scratch_shapes=[pltpu.VMEM((nj, TM, tn), BF16)],
        compiler_params=_params(("arbitrary", "arbitrary")),
        name="merge",
    )(h, mods, x2, *branches,
      w_gate_t, w_gate_t, w_gate_t, w_gate_t, b_gate, b_gate, b_gate, b_gate,
      w_branch_t, w_branch_t, w_branch_t, w_branch_t, w_out_t)


def _ffn_kernel(x_ref, xp_ref, xn_ref, g_ref, sc_ref, sh_ref, gate_ref,
                wa_ref, wb_ref, ca_ref, cb_ref, wd_ref,
                o_ref, h_ref, ua0_ref, ub0_ref, ua1_ref, ub1_ref, act_ref, *, tps, nf):
    i = pl.program_id(0)
    j = pl.program_id(1)
    tm = x_ref.shape[0]
    tf = ua0_ref.shape[1]
    ti = i % tps

    ch = FFN_CHUNK
    ext = ch + 2 * 8

    bufs = ((ua0_ref, ub0_ref), (ua1_ref, ub1_ref))

    def up(slot):
        h = h_ref[...]
        bufs[slot][0][...] = jnp.dot(h, wa_ref[0], preferred_element_type=F32)
        bufs[slot][1][...] = jnp.dot(h, wb_ref[0], preferred_element_type=F32)

    def conv(u_ref, c_ref, r0):
        u = u_ref[r0 + HALO - 8:r0 + HALO - 8 + ext, :]
        prev = pltpu.roll(u, 1, 0)[8:8 + ch]
        nxt = pltpu.roll(u, ext - 1, 0)[8:8 + ch]
        return (prev * c_ref[0, 0:1, :] + u[8:8 + ch] * c_ref[0, 1:2, :]
                + nxt * c_ref[0, 2:3, :] + c_ref[0, 3:4, :])

    def activate(slot, tile):
        for r0 in range(0, tm, ch):
            a = conv(bufs[slot][0], ca_ref, r0)
            b = conv(bufs[slot][1], cb_ref, r0)
            act_ref[tile, r0:r0 + ch, :] = (jax.nn.gelu(a, approximate=True) * b).astype(BF16)

    def down():
        out = jnp.dot(act_ref[0], wd_ref[0, 0:tf, :], preferred_element_type=F32)
        for kk in range(1, nf):
            out = out + jnp.dot(act_ref[kk], wd_ref[0, kk * tf:(kk + 1) * tf, :],
                                preferred_element_type=F32)
        res = gate_ref[0] * out
        tn = o_ref.shape[1]
        for c in range(x_ref.shape[1] // tn):
            @pl.when(j - nf == c)
            def _():
                o_ref[...] = x_ref[:, c * tn:(c + 1) * tn] + res

    @pl.when(j == 0)
    def _():
        g = g_ref[0]
        sc = sc_ref[0]
        sh = sh_ref[0]
        _norm_rows(xp_ref, h_ref, 0, g, sc, sh, keep=ti != 0)
        _norm_rows(x_ref, h_ref, HALO, g, sc, sh)
        _norm_rows(xn_ref, h_ref, HALO + tm, g, sc, sh, keep=ti != tps - 1)
        up(0)

    for parity in range(2):
        @pl.when((j >= 1) & (j < nf) & (j % 2 == parity))
        def _():
            activate(1 - parity, j - 1)
            up(parity)

    @pl.when(j == nf)
    def _():
        activate((nf - 1) % 2, nf - 1)
        down()

    @pl.when(j > nf)
    def _():
        down()


def _ffn(x2, layer, g, mods, w_up_t, conv_p, w_down_t, seq):
    rows, d = x2.shape
    tm = TM_FFN
    tps = seq // tm
    nf = D_FF // TF
    tn = TN_DOWN
    nd = d // tn
    hb = tm // HALO
    nhalo = rows // HALO

    def first(j):
        return jnp.minimum(j, nf - 1)

    def second(j):
        return jnp.maximum(j - nf, 0)

    def lagged(j):
        return jnp.clip(j - 1, 0, nf - 1)

    return pl.pallas_call(
        functools.partial(_ffn_kernel, tps=tps, nf=nf),
        out_shape=jax.ShapeDtypeStruct((rows, d), F32),
        grid=(rows // tm, nf + nd),
        in_specs=[
            pl.BlockSpec((tm, d), lambda i, j: (i, 0), pipeline_mode=pl.Buffered(1)),
            pl.BlockSpec((HALO, d), lambda i, j: (jnp.maximum(i * hb - 1, 0), 0)),
            pl.BlockSpec((HALO, d), lambda i, j: (jnp.minimum((i + 1) * hb, nhalo - 1), 0)),
            pl.BlockSpec((1, 1, d), lambda i, j: (layer, 0, 0)),
            pl.BlockSpec((1, 1, d), lambda i, j: ((i // tps) * 6 + 4, 0, 0)),
            pl.BlockSpec((1, 1, d), lambda i, j: ((i // tps) * 6 + 3, 0, 0)),
            pl.BlockSpec((1, 1, tn), lambda i, j: ((i // tps) * 6 + 5, 0, second(j))),
            pl.BlockSpec((1, d, TF), lambda i, j: (layer, 0, first(j))),
            pl.BlockSpec((1, d, TF), lambda i, j: (layer, 0, nf + first(j))),
            pl.BlockSpec((1, 8, TF), lambda i, j: (layer, 0, lagged(j))),
            pl.BlockSpec((1, 8, TF), lambda i, j: (layer, 0, nf + lagged(j))),
            pl.BlockSpec((1, D_FF, tn), lambda i, j: (layer, 0, second(j))),
        ],
        out_specs=pl.BlockSpec((tm, tn), lambda i, j: (i, second(j))),
        scratch_shapes=[pltpu.VMEM((tm + 2 * HALO, d), BF16),
                        pltpu.VMEM((tm + 2 * HALO, TF), F32),
                        pltpu.VMEM((tm + 2 * HALO, TF), F32),
                        pltpu.VMEM((tm + 2 * HALO, TF), F32),
                        pltpu.VMEM((tm + 2 * HALO, TF), F32),
                        pltpu.VMEM((nf, tm, TF), BF16)],
        compiler_params=_params(("arbitrary", "arbitrary")),
        name="conv_ffn",
    )(x2, x2, x2, g, mods, mods, mods, w_up_t, w_up_t, conv_p, conv_p, w_down_t)


def _final_kernel(x_ref, g_ref, o_ref):
    x = x_ref[...]
    ms = jnp.mean(x * x, axis=-1, keepdims=True)
    o_ref[...] = x * lax.rsqrt(ms + EPS) * g_ref[...]


def _final_norm(x2, g_final):
    rows, d = x2.shape
    tr = 256
    return pl.pallas_call(
        _final_kernel,
        out_shape=jax.ShapeDtypeStruct((rows, d), F32),
        grid=(rows // tr,),
        in_specs=[pl.BlockSpec((tr, d), lambda i: (i, 0)), pl.BlockSpec((1, d), lambda i: (0, 0))],
        out_specs=pl.BlockSpec((tr, d), lambda i: (i, 0)),
        compiler_params=_params(("arbitrary",)),
        name="final_norm",
    )(x2, g_final)


def _rope_tables(seq):
    half = RET_HEAD_DIM // 2
    inv = ROPE_BASE ** (-jnp.arange(half, dtype=F32) / half)
    ang = jnp.arange(seq, dtype=F32)[:, None] * inv[None, :]
    cos = jnp.cos(ang)
    sin = jnp.sin(ang)
    return jnp.concatenate([cos, cos], axis=1), jnp.concatenate([-sin, sin], axis=1)


def _trunk(x, mods, wts):
    batch, seq, d = x.shape
    x2 = x.reshape(batch * seq, d)
    cos_t, sin_t = _rope_tables(seq)
    fc = _fourier_consts(seq)
    for l in range(DEPTH):
        m = mods[l]
        pool_in, four_in, ret_in, na_in, h = _in_proj(x2, l, wts["g_mix"], m, cos_t, sin_t, wts["w_in"], seq)
        branches = (
            _pool_branch(pool_in, l, wts["pool_w"], wts["pool_scale"], seq),
            _fourier_branch(four_in, fc, batch, seq),
            _ret_branch(ret_in, l, wts["ret_decay_f"], wts["ret_decay_b"], batch, seq),
            _na_branch(na_in, l, wts["na_bias"], batch, seq),
        )
        x2 = _merge(x2, h, l, m, branches, wts["w_gate"], wts["b_gate"], wts["w_branch"], wts["w_out"], seq)
        x2 = _ffn(x2, l, wts["g_ffn"], m, wts["w_up"], wts["conv_p"], wts["w_down"], seq)
    return _final_norm(x2, wts["g_final"]).reshape(batch, seq, d)


def kernel(x_prompt, x_sample, c_prompt, c_sample, w_ada, b_ada, g_mix, w_in, pool_w, pool_scale,
           ret_decay_f, ret_decay_b, na_rpb, w_branch, w_gate, b_gate, w_out, g_ffn, w_up, conv_w,
           conv_b, w_down, g_final):
    d = D_MODEL
    nb_p = c_prompt.shape[0]
    nb_s = c_sample.shape[0]
    c8 = jnp.concatenate([c_prompt, c_sample, jnp.zeros((8 - nb_p - nb_s, d), F32)], axis=0)
    mod_all = _ada(c8, w_ada, b_ada)
    mods_p = mod_all[:, :nb_p].reshape(DEPTH, nb_p * 6, 1, d)
    mods_s = mod_all[:, nb_p:nb_p + nb_s].reshape(DEPTH, nb_s * 6, 1, d)
    decay_shape = (DEPTH, RET_HEADS, 1, RET_HEAD_DIM)
    wts = dict(
        g_mix=g_mix.reshape(DEPTH, 1, d),
        w_in=w_in.astype(BF16),
        pool_w=pool_w.astype(BF16),
        pool_scale=pool_scale.reshape(DEPTH, 1, BRANCH_WIDTH),
        ret_decay_f=jnp.broadcast_to(ret_decay_f.astype(F32)[:, :, None, None], decay_shape),
        ret_decay_b=jnp.broadcast_to(ret_decay_b.astype(F32)[:, :, None, None], decay_shape),
        na_bias=_na_bias(na_rpb),
        w_branch=w_branch.astype(BF16),
        w_gate=w_gate.astype(BF16),
        b_gate=b_gate.reshape(DEPTH, 1, N_BRANCH * d),
        w_out=w_out.astype(BF16),
        g_ffn=g_ffn.reshape(DEPTH, 1, d),
        w_up=w_up.astype(BF16),
        conv_p=jnp.concatenate([conv_w, conv_b[:, None, :], jnp.zeros((DEPTH, 4, 2 * D_FF), F32)], axis=1),
        w_down=w_down.astype(BF16),
        g_final=g_final.reshape(1, d),
    )
    y_prompt = _trunk(x_prompt, mods_p, wts)
    y_sample = _trunk(x_sample, mods_s, wts)
    return (y_prompt, y_sample)
```

```python
import functools
import math

import numpy as np
import jax
import jax.numpy as jnp
from jax import lax
from jax.experimental import pallas as pl
from jax.experimental.pallas import tpu as pltpu

F32 = jnp.float32
BF16 = jnp.bfloat16

D_MODEL = 2048
DEPTH = 4
GRID_W = 64
BRANCH_WIDTH = 512
N_BRANCH = 4
POOL_WINDOWS = (2, 4, 8, 16)
POOL_GROUP = 128
FOURIER_GROUP = 128
RET_HEADS = 4
RET_HEAD_DIM = 128
ROPE_BASE = 10000.0
NA_HEADS = 8
NA_HEAD_DIM = 64
NA_ROWS_MAX = 8
NA_COLS = 16
D_FF = 5632
EPS = 1e-6
NEG_INF = -1e30
IN_WIDTH = 9 * BRANCH_WIDTH

VMEM_LIMIT_BYTES = 56 * 1024 * 1024
VMEM_LIMIT_FFN_BYTES = 58 * 1024 * 1024
BF16_SUBLANES = 16

TM = 1024
TN_IN = 512
TN_MERGE = 256
TM_FFN = 1024
TF = 512
TN_DOWN = 256
HALO = BF16_SUBLANES
NORM_CHUNK = 32
NORM_UNROLL = 4
FFN_CHUNK = 32
POOL_TILE = 512
POOL_HALO = 8
F_T2 = 128
F1_TILE = 8
F2_TILE = 8
RET_CHUNK = 256
RET_GROUP = 4
NA_ROWS_PER_STEP = 8
NA_ROW_GROUP = 4


def _params(sem, vmem_limit=VMEM_LIMIT_BYTES):
    return pltpu.CompilerParams(dimension_semantics=sem, vmem_limit_bytes=vmem_limit)


def _norm_rows(x_ref, dst_ref, dst_off, g, scale, shift, keep=None):
    rows = x_ref.shape[0]
    ch = min(rows, NORM_CHUNK)
    gain = g * (1.0 + scale)

    def body(c, carry):
        r0 = pl.multiple_of(c * ch, ch)
        x = x_ref[pl.ds(r0, ch), :]
        ms = jnp.mean(x * x, axis=-1, keepdims=True)
        h = (x * lax.rsqrt(ms + EPS)) * gain + shift
        if keep is not None:
            h = jnp.where(keep, h, 0.0)
        dst_ref[pl.ds(dst_off + r0, ch), :] = h.astype(BF16)
        return carry

    lax.fori_loop(0, rows // ch, body, 0, unroll=min(NORM_UNROLL, rows // ch))


def _ada_kernel(c_ref, w_ref, b_ref, o_ref):
    c = c_ref[...]
    s = (c * jax.nn.sigmoid(c)).astype(BF16)
    o_ref[0] = jnp.dot(s, w_ref[0].astype(BF16), preferred_element_type=F32) + b_ref[0]


def _ada(c8, w_ada, b_ada):
    depth, d, n = w_ada.shape
    tn = 1024
    return pl.pallas_call(
        _ada_kernel,
        out_shape=jax.ShapeDtypeStruct((depth, 8, n), F32),
        grid=(depth, n // tn),
        in_specs=[
            pl.BlockSpec((8, d), lambda l, j: (0, 0)),
            pl.BlockSpec((1, d, tn), lambda l, j: (l, 0, j)),
            pl.BlockSpec((1, 1, tn), lambda l, j: (l, 0, j)),
        ],
        out_specs=pl.BlockSpec((1, 8, tn), lambda l, j: (l, 0, j)),
        compiler_params=_params(("arbitrary", "arbitrary")),
        name="ada",
    )(c8, w_ada, b_ada.reshape(depth, 1, n))


def _rope_tile(acc, cos, sin, scale):
    parts = []
    for hh in range(acc.shape[1] // RET_HEAD_DIM):
        ch = acc[:, hh * RET_HEAD_DIM:(hh + 1) * RET_HEAD_DIM]
        parts.append((ch * cos + pltpu.roll(ch, RET_HEAD_DIM // 2, 1) * sin) * scale)
    return jnp.concatenate(parts, axis=1)


def _in_kernel(xa_ref, xb_ref, g_ref, sc_ref, sh_ref, cos_ref, sin_ref, w_ref,
               pool_ref, four_ref, ret_ref, na_ref, h_ref):
    j = pl.program_id(1)

    @pl.when(j == 0)
    def _():
        _norm_rows(xa_ref, h_ref, 0, g_ref[0], sc_ref[0], sh_ref[0])
        _norm_rows(xb_ref, h_ref, xa_ref.shape[0], g_ref[0], sc_ref[0], sh_ref[0])

    def proj():
        return jnp.dot(h_ref[...], w_ref[0], preferred_element_type=F32)

    @pl.when(j == 0)
    def _():
        pool_ref[...] = proj()

    @pl.when(j == 1)
    def _():
        four_ref[...] = proj()

    @pl.when(j == 2)
    def _():
        ret_ref[...] = _rope_tile(proj(), cos_ref[...], sin_ref[...], 1.0).astype(BF16)

    @pl.when(j == 3)
    def _():
        ret_ref[...] = _rope_tile(proj(), cos_ref[...], sin_ref[...], RET_HEAD_DIM ** -0.5).astype(BF16)

    @pl.when((j == 4) | (j == 5))
    def _():
        ret_ref[...] = proj().astype(BF16)

    @pl.when(j == 6)
    def _():
        na_ref[...] = (proj() * (NA_HEAD_DIM ** -0.5)).astype(BF16)

    @pl.when(j >= 7)
    def _():
        na_ref[...] = proj().astype(BF16)


def _in_proj(x2, layer, g, mods, cos_t, sin_t, w_in_t, seq):
    rows, d = x2.shape
    tps = seq // TM
    nj = IN_WIDTH // TN_IN
    last_half = 2 * (rows // TM)

    def mod_spec(k):
        return pl.BlockSpec((1, 1, d), lambda i, j: ((i // tps) * 6 + k, 0, 0))

    return pl.pallas_call(
        _in_kernel,
        out_shape=(
            jax.ShapeDtypeStruct((rows, BRANCH_WIDTH), F32),
            jax.ShapeDtypeStruct((rows, BRANCH_WIDTH), F32),
            jax.ShapeDtypeStruct((rows, 4 * BRANCH_WIDTH), BF16),
            jax.ShapeDtypeStruct((rows, 3 * BRANCH_WIDTH), BF16),
            jax.ShapeDtypeStruct((rows, d), BF16),
        ),
        grid=(rows // TM, nj),
        in_specs=[
            pl.BlockSpec((TM // 2, d), lambda i, j: (jnp.minimum(2 * (i + j // (nj - 1)), last_half - 2), 0)),
            pl.BlockSpec((TM // 2, d), lambda i, j: (2 * i + 1, 0)),
            pl.BlockSpec((1, 1, d), lambda i, j: (layer, 0, 0)),
            mod_spec(1),
            mod_spec(0),
            pl.BlockSpec((TM, RET_HEAD_DIM), lambda i, j: (i % tps, 0)),
            pl.BlockSpec((TM, RET_HEAD_DIM), lambda i, j: (i % tps, 0)),
            pl.BlockSpec((1, d, TN_IN), lambda i, j: (layer, 0, j)),
        ],
        out_specs=(
            pl.BlockSpec((TM, TN_IN), lambda i, j: (i, 0)),
            pl.BlockSpec((TM, TN_IN), lambda i, j: (i, 0)),
            pl.BlockSpec((TM, TN_IN), lambda i, j: (i, jnp.clip(j - 2, 0, 3))),
            pl.BlockSpec((TM, TN_IN), lambda i, j: (i, jnp.clip(j - 6, 0, 2))),
            pl.BlockSpec((TM, d), lambda i, j: (i, 0)),
        ),
        compiler_params=_params(("arbitrary", "arbitrary")),
        name="in_proj",
    )(x2, x2, g, mods, mods, cos_t, sin_t, w_in_t)


def _pool_kernel(a_ref, ap_ref, an_ref, w_ref, s_ref, o_ref, ext_ref, *, tps, seq):
    tp = a_ref.shape[0]
    ti = pl.program_id(0) % tps
    ext_ref[0:POOL_HALO, :] = jnp.where(ti == 0, 0.0, ap_ref[...])
    ext_ref[POOL_HALO:POOL_HALO + tp, :] = a_ref[...]
    ext_ref[POOL_HALO + tp:2 * POOL_HALO + tp, :] = jnp.where(ti == tps - 1, 0.0, an_ref[...])
    t = ti * tp + lax.broadcasted_iota(jnp.int32, (tp, 1), 0)
    for gi, w in enumerate(POOL_WINDOWS):
        cols = slice(gi * POOL_GROUP, (gi + 1) * POOL_GROUP)
        half = w // 2
        acc = ext_ref[pl.ds(POOL_HALO - half, tp), cols]
        for k in range(-half + 1, half):
            acc = acc + ext_ref[pl.ds(POOL_HALO + k, tp), cols]
        count = (jnp.minimum(t + half, seq) - jnp.maximum(t - half, 0)).astype(F32)
        pooled = acc / count - a_ref[:, cols]
        y = jnp.dot(pooled.astype(BF16), w_ref[0, gi], preferred_element_type=F32) * s_ref[0, :, cols]
        o_ref[:, cols] = y.astype(BF16)


def _pool_branch(a, layer, pool_w, pool_scale, seq):
    rows, w = a.shape
    tp = POOL_TILE
    tps = seq // tp
    hb = tp // POOL_HALO
    nhalo = rows // POOL_HALO
    return pl.pallas_call(
        functools.partial(_pool_kernel, tps=tps, seq=seq),
        out_shape=jax.ShapeDtypeStruct((rows, w), BF16),
        grid=(rows // tp,),
        in_specs=[
            pl.BlockSpec((tp, w), lambda i: (i, 0)),
            pl.BlockSpec((POOL_HALO, w), lambda i: (jnp.maximum(i * hb - 1, 0), 0)),
            pl.BlockSpec((POOL_HALO, w), lambda i: (jnp.minimum((i + 1) * hb, nhalo - 1), 0)),
            pl.BlockSpec((1, len(POOL_WINDOWS), POOL_GROUP, POOL_GROUP), lambda i: (layer, 0, 0, 0)),
            pl.BlockSpec((1, 1, w), lambda i: (layer, 0, 0)),
        ],
        out_specs=pl.BlockSpec((tp, w), lambda i: (i, 0)),
        scratch_shapes=[pltpu.VMEM((tp + 2 * POOL_HALO, w), F32)],
        compiler_params=_params(("arbitrary",)),
        name="pool",
    )(a, a, a, pool_w, pool_scale)


def _f1_kernel(x_ref, k1_ref, tc_ref, ts_ref, yr_ref, yi_ref):
    _, n1, t, w = x_ref.shape
    x = x_ref[0].reshape(n1 * t, w).astype(BF16)
    y = jnp.dot(k1_ref[...], x, preferred_element_type=F32)
    yr = y[:n1 * t]
    yi = y[n1 * t:]
    reps = w // tc_ref.shape[2]
    c = jnp.concatenate([tc_ref[...].reshape(n1 * t, w // reps)] * reps, axis=1)
    s = jnp.concatenate([ts_ref[...].reshape(n1 * t, w // reps)] * reps, axis=1)
    yr_ref[0] = (yr * c + yi * s).reshape(n1, t, w)
    yi_ref[0] = (yi * c - yr * s).reshape(n1, t, w)


def _f2_kernel(yr_ref, yi_ref, ma_ref, mb_ref, cc_ref, cs_ref, o_ref, zr_ref, zi_ref, *, scale):
    n2 = F_T2
    for s in range(F2_TILE):
        rows = slice(s * n2, (s + 1) * n2)
        z = (jnp.dot(ma_ref[...], yr_ref[0, rows, :].astype(BF16), preferred_element_type=F32)
             + jnp.dot(mb_ref[...], yi_ref[0, rows, :].astype(BF16), preferred_element_type=F32))
        zr_ref[rows, :] = z[:n2].astype(BF16)
        zi_ref[rows, :] = z[n2:].astype(BF16)
    o = (jnp.dot(zr_ref[...], cc_ref[...], preferred_element_type=F32)
         + jnp.dot(zi_ref[...], cs_ref[...], preferred_element_type=F32)) * scale
    for s in range(F2_TILE):
        o_ref[0, :, s, :] = o[s * n2:(s + 1) * n2]


def _dft_cos_sin(n):
    k = np.arange(n)
    ang = 2.0 * np.pi * ((k[:, None] * k[None, :]) % n) / n
    return np.cos(ang), np.sin(ang)


def _fourier_consts(seq):
    n2 = F_T2
    n1 = seq // n2
    c1, s1 = _dft_cos_sin(n1)
    c2, s2 = _dft_cos_sin(n2)
    cg, sg = _dft_cos_sin(FOURIER_GROUP)
    eye = np.eye(BRANCH_WIDTH // FOURIER_GROUP)
    k1 = jnp.asarray(np.kron(np.concatenate([c1, -s1], axis=0), np.eye(F1_TILE)), BF16)
    ma = jnp.asarray(np.concatenate([c2, -s2], axis=0), BF16)
    mb = jnp.asarray(np.concatenate([s2, c2], axis=0), BF16)
    cc = jnp.asarray(np.kron(eye, cg), BF16)
    cs = jnp.asarray(np.kron(eye, sg), BF16)
    prod = (jnp.arange(n1, dtype=jnp.int32)[:, None] * jnp.arange(n2, dtype=jnp.int32)[None, :]) % seq
    ang = prod.astype(F32) * (2.0 * math.pi / seq)
    shape = (n1, n2, FOURIER_GROUP)
    tc = jnp.broadcast_to(jnp.cos(ang)[:, :, None], shape)
    ts = jnp.broadcast_to(jnp.sin(ang)[:, :, None], shape)
    return dict(k1=k1, ma=ma, mb=mb, cc=cc, cs=cs, tc=tc, ts=ts)


def _fourier_branch(f, fc, batch, seq):
    w = BRANCH_WIDTH
    n2 = F_T2
    n1 = seq // n2
    x4 = f.reshape(batch, n1, n2, w)
    y_spec = pl.BlockSpec((1, n1, F1_TILE, w), lambda j, b: (b, 0, j, 0))
    tw_spec = pl.BlockSpec((n1, F1_TILE, FOURIER_GROUP), lambda j, b: (0, j, 0))
    yr, yi = pl.pallas_call(
        _f1_kernel,
        out_shape=(jax.ShapeDtypeStruct((batch, n1, n2, w), F32),) * 2,
        grid=(n2 // F1_TILE, batch),
        in_specs=[
            y_spec,
            pl.BlockSpec((2 * n1 * F1_TILE, n1 * F1_TILE), lambda j, b: (0, 0)),
            tw_spec,
            tw_spec,
        ],
        out_specs=(y_spec, y_spec),
        compiler_params=_params(("arbitrary", "arbitrary")),
        name="fourier1",
    )(x4, fc["k1"], fc["tc"], fc["ts"])
    yr = yr.reshape(batch, n1 * n2, w)
    yi = yi.reshape(batch, n1 * n2, w)
    rows = F2_TILE * n2
    out = pl.pallas_call(
        functools.partial(_f2_kernel, scale=1.0 / math.sqrt(seq * FOURIER_GROUP)),
        out_shape=jax.ShapeDtypeStruct((batch, n2, n1, w), F32),
        grid=(batch, n1 // F2_TILE),
        in_specs=[
            pl.BlockSpec((1, rows, w), lambda b, j: (b, j, 0)),
            pl.BlockSpec((1, rows, w), lambda b, j: (b, j, 0)),
            pl.BlockSpec((2 * n2, n2), lambda b, j: (0, 0)),
            pl.BlockSpec((2 * n2, n2), lambda b, j: (0, 0)),
            pl.BlockSpec((w, w), lambda b, j: (0, 0)),
            pl.BlockSpec((w, w), lambda b, j: (0, 0)),
        ],
        out_specs=pl.BlockSpec((1, n2, F2_TILE, w), lambda b, j: (b, 0, j, 0)),
        scratch_shapes=[pltpu.VMEM((rows, w), BF16), pltpu.VMEM((rows, w), BF16)],
        compiler_params=_params(("arbitrary", "arbitrary")),
        name="fourier2",
    )(yr, yi, fc["ma"], fc["mb"], fc["cc"], fc["cs"])
    return out.reshape(batch * seq, w)


def _log_sigmoid(x):
    return jnp.minimum(x, 0.0) - jnp.log1p(jnp.exp(-jnp.abs(x)))


def _ret_kernel(q_ref, k_ref, v_ref, g_ref, df_ref, db_ref, o_ref, kv_ref, r_ref, *, seq):
    c = RET_CHUNK
    d = RET_HEAD_DIM
    nc = seq // c
    lgf = _log_sigmoid(df_ref[0, 0])
    lgb = _log_sigmoid(db_ref[0, 0])
    ii = lax.broadcasted_iota(jnp.int32, (c, 1), 0).astype(F32)
    kdf = jnp.exp(lgf * (c - 1.0 - ii))
    kdb = jnp.exp(lgb * ii)
    qdf = jnp.exp(lgf * (ii + 1.0))
    qdb = jnp.exp(lgb * (c - ii))
    diff = (lax.broadcasted_iota(jnp.int32, (c, c), 0)
            - lax.broadcasted_iota(jnp.int32, (c, c), 1)).astype(F32)
    lgf_c = jnp.concatenate([lgf] * (c // d), axis=1)
    lgb_c = jnp.concatenate([lgb] * (c // d), axis=1)
    dmat = jnp.where(diff >= 0.0, jnp.exp(lgf_c * jnp.maximum(diff, 0.0)),
                     jnp.exp(lgb_c * jnp.maximum(-diff, 0.0)))
    chunk_f = jnp.exp(lgf * float(c))
    chunk_b = jnp.exp(lgb * float(c))

    def kv_body(n, carry):
        rows = pl.ds(pl.multiple_of(n * c, c), c)
        kc = k_ref[0, rows, :].astype(F32)
        kd = jnp.concatenate([kc * kdf, kc * kdb], axis=1).astype(BF16)
        kv_ref[n] = lax.dot_general(kd, v_ref[0, rows, :], (((0,), (0,)), ((), ())),
                                    preferred_element_type=F32)
        return carry

    lax.fori_loop(0, nc, kv_body, 0, unroll=2)

    def fwd_body(n, state):
        r_ref[n, 0:d, :] = state.astype(BF16)
        return chunk_f * state + kv_ref[n, 0:d, :]

    lax.fori_loop(0, nc, fwd_body, jnp.zeros((d, d), F32))

    def bwd_body(m, state):
        n = nc - 1 - m
        r_ref[n, d:2 * d, :] = state.astype(BF16)
        return chunk_b * state + kv_ref[n, d:2 * d, :]

    lax.fori_loop(0, nc, bwd_body, jnp.zeros((d, d), F32))

    def out_body(t, carry):
        staged = []
        for n in [RET_GROUP * t + u for u in range(RET_GROUP)]:
            rows = pl.ds(pl.multiple_of(n * c, c), c)
            qb = q_ref[0, rows, :]
            s = lax.dot_general(qb, k_ref[0, rows, :], (((1,), (1,)), ((), ())),
                                preferred_element_type=F32)
            qc = qb.astype(F32)
            qd = jnp.concatenate([qc * qdf, qc * qdb], axis=1).astype(BF16)
            staged.append((rows, s, jnp.dot(qd, r_ref[n], preferred_element_type=F32)))
        for rows, s, cross in staged:
            o = jnp.dot((s * dmat).astype(BF16), v_ref[0, rows, :], preferred_element_type=F32) + cross
            mu = jnp.mean(o, axis=-1, keepdims=True)
            var = jnp.mean(jnp.square(o - mu), axis=-1, keepdims=True)
            on = (o - mu) * lax.rsqrt(var + EPS)
            g = g_ref[0, rows, :].astype(F32)
            o_ref[0, rows, :] = (on * (g * jax.nn.sigmoid(g))).astype(BF16)
        return carry

    lax.fori_loop(0, nc // RET_GROUP, out_body, 0)


def _ret_branch(ret, layer, decay_f, decay_b, batch, seq):
    d = RET_HEAD_DIM
    nc = seq // RET_CHUNK
    r3 = ret.reshape(batch, seq, 4 * BRANCH_WIDTH)

    def sec(k):
        return pl.BlockSpec((1, seq, d), lambda b, h: (b, 0, k * RET_HEADS + h))

    decay_spec = pl.BlockSpec((1, 1, 1, d), lambda b, h: (layer, h, 0, 0))
    out = pl.pallas_call(
        functools.partial(_ret_kernel, seq=seq),
        out_shape=jax.ShapeDtypeStruct((batch, seq, BRANCH_WIDTH), BF16),
        grid=(batch, RET_HEADS),
        in_specs=[sec(0), sec(1), sec(2), sec(3), decay_spec, decay_spec],
        out_specs=pl.BlockSpec((1, seq, d), lambda b, h: (b, 0, h)),
        scratch_shapes=[pltpu.VMEM((nc, 2 * d, d), F32), pltpu.VMEM((nc, 2 * d, d), BF16)],
        compiler_params=_params(("arbitrary", "arbitrary")),
        name="retention",
    )(r3, r3, r3, r3, decay_f, decay_b)
    return out.reshape(batch * seq, BRANCH_WIDTH)


def _na_kernel(q_ref, k0_ref, k1_ref, k2_ref, v0_ref, v1_ref, v2_ref, bias_ref, o_ref,
               kw_ref, vw_ref, *, grid_rows):
    rr = NA_ROWS_PER_STEP
    gw = GRID_W
    blk = rr * gw
    kr = NA_ROWS_MAX
    m = pl.program_id(1)
    for idx, (kref, vref) in enumerate(((k0_ref, v0_ref), (k1_ref, v1_ref), (k2_ref, v2_ref))):
        kw_ref[idx * blk:(idx + 1) * blk, :] = kref[0]
        vw_ref[idx * blk:(idx + 1) * blk, :] = vref[0]
    lane = lax.broadcasted_iota(jnp.int32, (gw, 2 * NA_HEAD_DIM), 1)
    low = lane < NA_HEAD_DIM

    pair_cols = [slice(jp * 2 * NA_HEAD_DIM, (jp + 1) * 2 * NA_HEAD_DIM) for jp in range(NA_HEADS // 2)]

    def window(r):
        r_glob = m * rr + r
        rs = jnp.clip(r_glob - kr // 2, 0, grid_rows - kr)
        krows = pl.ds(pl.multiple_of((rs - (m - 1) * rr) * gw, gw), kr * gw)
        qrows = pl.ds(pl.multiple_of(r * gw, gw), gw)
        return qrows, krows, r_glob - rs

    def scores(r):
        qrows, krows, didx = window(r)
        out = []
        for jp, cols in enumerate(pair_cols):
            q2 = q_ref[0, qrows, cols]
            zero = jnp.zeros_like(q2)
            qq = jnp.concatenate([jnp.where(low, q2, zero), jnp.where(low, zero, q2)], axis=0)
            s = lax.dot_general(qq, kw_ref[krows, cols], (((1,), (1,)), ((), ())),
                                preferred_element_type=F32)
            out.append(s + bias_ref[0, didx, jp])
        return out

    def softmax(ss):
        out = []
        for s in ss:
            p = jnp.exp(s - jnp.max(s, axis=-1, keepdims=True))
            out.append((p.astype(BF16), jnp.sum(p, axis=-1, keepdims=True)))
        return out

    def values(r, ps):
        qrows, krows, _ = window(r)
        for (p, l), cols in zip(ps, pair_cols):
            o = jnp.dot(p, vw_ref[krows, cols], preferred_element_type=F32) / l
            o_ref[0, qrows, cols] = jnp.where(low, o[:gw], o[gw:]).astype(BF16)

    def group_body(t, carry):
        rows = [NA_ROW_GROUP * t + u for u in range(NA_ROW_GROUP)]
        staged = [scores(r) for r in rows]
        for r, ss in zip(rows, staged):
            values(r, softmax(ss))
        return carry

    lax.fori_loop(0, rr // NA_ROW_GROUP, group_body, 0)


def _na_bias(rpb):
    rpb = rpb.astype(F32)
    per_q = []
    for qc in range(GRID_W):
        cs = min(max(qc - NA_COLS // 2, 0), GRID_W - NA_COLS)
        lo = (NA_COLS - 1) - qc + cs
        seg = rpb[..., lo:lo + NA_COLS]
        per_q.append(jnp.pad(seg, ((0, 0),) * 3 + ((cs, GRID_W - NA_COLS - cs),), constant_values=NEG_INF))
    toep = jnp.stack(per_q, axis=3)
    per_dl = []
    for dl in range(NA_ROWS_MAX):
        lo = NA_ROWS_MAX - 1 - dl
        win = toep[:, :, lo:lo + NA_ROWS_MAX]
        per_dl.append(jnp.transpose(win, (0, 1, 3, 2, 4)).reshape(
            rpb.shape[0], NA_HEADS, GRID_W, NA_ROWS_MAX * GRID_W))
    table = jnp.stack(per_dl, axis=1)
    return table.reshape(rpb.shape[0], NA_ROWS_MAX, NA_HEADS // 2, 2 * GRID_W, NA_ROWS_MAX * GRID_W)


def _na_branch(na, layer, bias, batch, seq):
    w = BRANCH_WIDTH
    grid_rows = seq // GRID_W
    rr = NA_ROWS_PER_STEP
    blk = rr * GRID_W
    nblk = grid_rows // rr
    n3 = na.reshape(batch, seq, 3 * w)

    def kv_spec(col, shift):
        return pl.BlockSpec((1, blk, w), lambda b, m: (b, jnp.clip(m + shift, 0, nblk - 1), col))

    out = pl.pallas_call(
        functools.partial(_na_kernel, grid_rows=grid_rows),
        out_shape=jax.ShapeDtypeStruct((batch, seq, w), BF16),
        grid=(batch, nblk),
        in_specs=[pl.BlockSpec((1, blk, w), lambda b, m: (b, m, 0)),
                  kv_spec(1, -1), kv_spec(1, 0), kv_spec(1, 1),
                  kv_spec(2, -1), kv_spec(2, 0), kv_spec(2, 1),
                  pl.BlockSpec((1,) + bias.shape[1:], lambda b, m: (layer, 0, 0, 0, 0))],
        out_specs=pl.BlockSpec((1, blk, w), lambda b, m: (b, m, 0)),
        scratch_shapes=[pltpu.VMEM((3 * blk, w), BF16), pltpu.VMEM((3 * blk, w), BF16)],
        compiler_params=_params(("arbitrary", "arbitrary")),
        name="nbr_attn",
    )(n3, n3, n3, n3, n3, n3, n3, bias)
    return out.reshape(batch * seq, w)


def _merge_kernel(h_ref, gate_ref, xs_ref,
                  br0, br1, br2, br3, wg0, wg1, wg2, wg3, bg0, bg1, bg2, bg3,
                  wb0, wb1, wb2, wb3, wo_ref, o_ref, m_ref, *, nj):
    j = pl.program_id(1)

    @pl.when(j < nj)
    def _():
        acc = None
        for br, wg, bg, wb in ((br0, wg0, bg0, wb0), (br1, wg1, bg1, wb1),
                               (br2, wg2, bg2, wb2), (br3, wg3, bg3, wb3)):
            gate = jax.nn.sigmoid(jnp.dot(h_ref[...], wg[0], preferred_element_type=F32) + bg[0])
            term = gate * jnp.dot(br[...].astype(BF16), wb[0, 0], preferred_element_type=F32)
            acc = term if acc is None else acc + term
        m_ref[j] = acc.astype(BF16)

    @pl.when(j >= nj)
    def _():
        tn = wo_ref.shape[2]
        out = jnp.dot(m_ref[0], wo_ref[0, 0:tn, :], preferred_element_type=F32)
        for kk in range(1, nj):
            out = out + jnp.dot(m_ref[kk], wo_ref[0, kk * tn:(kk + 1) * tn, :], preferred_element_type=F32)
        o_ref[...] = xs_ref[...] + gate_ref[0] * out


def _merge(x2, h, layer, mods, branches, w_gate_t, b_gate, w_branch_t, w_out_t, seq):
    rows, d = x2.shape
    tn = TN_MERGE
    nj = d // tn
    tps = seq // TM

    def first(j):
        return jnp.minimum(j, nj - 1)

    def second(j):
        return jnp.maximum(j - nj, 0)

    def gate_w_spec(b):
        return pl.BlockSpec((1, d, tn), lambda i, j: (layer, 0, b * nj + first(j)))

    def gate_b_spec(b):
        return pl.BlockSpec((1, 1, tn), lambda i, j: (layer, 0, b * nj + first(j)))

    def branch_w_spec(b):
        return pl.BlockSpec((1, 1, BRANCH_WIDTH, tn), lambda i, j: (layer, b, 0, first(j)))

    def ahead(i, j):
        return jnp.minimum(i + j // nj, rows // TM - 1)

    br_spec = pl.BlockSpec((TM, BRANCH_WIDTH), lambda i, j: (ahead(i, j), 0))
    in_specs = [
        pl.BlockSpec((TM, d), lambda i, j: (ahead(i, j), 0)),
        pl.BlockSpec((1, 1, tn), lambda i, j: ((i // tps) * 6 + 2, 0, second(j))),
        pl.BlockSpec((TM, tn), lambda i, j: (i, second(j))),
        br_spec, br_spec, br_spec, br_spec,
        gate_w_spec(0), gate_w_spec(1), gate_w_spec(2), gate_w_spec(3),
        gate_b_spec(0), gate_b_spec(1), gate_b_spec(2), gate_b_spec(3),
        branch_w_spec(0), branch_w_spec(1), branch_w_spec(2), branch_w_spec(3),
        pl.BlockSpec((1, d, tn), lambda i, j: (layer, 0, second(j))),
    ]
    return pl.pallas_call(
        functools.partial(_merge_kernel, nj=nj),
        out_shape=jax.ShapeDtypeStruct((rows, d), F32),
        grid=(rows // TM, 2 * nj),
        in_specs=in_specs,
        out_specs=pl.BlockSpec((TM, tn), lambda i, j: (i, second(j))),
        scratch_shapes=[pltpu.VMEM((nj, TM, tn), BF16)],
        compiler_params=_params(("arbitrary", "arbitrary")),
        name="merge",
    )(h, mods, x2, *branches,
      w_gate_t, w_gate_t, w_gate_t, w_gate_t, b_gate, b_gate, b_gate, b_gate,
      w_branch_t, w_branch_t, w_branch_t, w_branch_t, w_out_t)


def _ffn_kernel(xa_ref, xb_ref, xp_ref, xn_ref, g_ref, sc_ref, sh_ref, gate_ref, xs_ref,
                wa_ref, wb_ref, ca_ref, cb_ref, wd_ref,
                o_ref, h_ref, ua0_ref, ub0_ref, ua1_ref, ub1_ref, act_ref, *, tps, nf):
    i = pl.program_id(0)
    j = pl.program_id(1)
    tm = 2 * xa_ref.shape[0]
    tf = ua0_ref.shape[1]
    ti = i % tps

    ch = FFN_CHUNK
    ext = ch + 2 * 8

    bufs = ((ua0_ref, ub0_ref), (ua1_ref, ub1_ref))

    def up(slot):
        h = h_ref[...]
        bufs[slot][0][...] = jnp.dot(h, wa_ref[0], preferred_element_type=F32)
        bufs[slot][1][...] = jnp.dot(h, wb_ref[0], preferred_element_type=F32)

    def conv(u_ref, c_ref, r0):
        u = u_ref[r0 + HALO - 8:r0 + HALO - 8 + ext, :]
        prev = pltpu.roll(u, 1, 0)[8:8 + ch]
        nxt = pltpu.roll(u, ext - 1, 0)[8:8 + ch]
        return (prev * c_ref[0, 0:1, :] + u[8:8 + ch] * c_ref[0, 1:2, :]
                + nxt * c_ref[0, 2:3, :] + c_ref[0, 3:4, :])

    def activate(slot, tile):
        for r0 in range(0, tm, ch):
            a = conv(bufs[slot][0], ca_ref, r0)
            b = conv(bufs[slot][1], cb_ref, r0)
            act_ref[tile, r0:r0 + ch, :] = (jax.nn.gelu(a, approximate=True) * b).astype(BF16)

    def down():
        out = jnp.dot(act_ref[0], wd_ref[0, 0:tf, :], preferred_element_type=F32)
        for kk in range(1, nf):
            out = out + jnp.dot(act_ref[kk], wd_ref[0, kk * tf:(kk + 1) * tf, :],
                                preferred_element_type=F32)
        o_ref[...] = xs_ref[...] + gate_ref[0] * out

    @pl.when(j == 0)
    def _():
        g = g_ref[0]
        sc = sc_ref[0]
        sh = sh_ref[0]
        _norm_rows(xp_ref, h_ref, 0, g, sc, sh, keep=ti != 0)
        _norm_rows(xa_ref, h_ref, HALO, g, sc, sh)
        _norm_rows(xb_ref, h_ref, HALO + tm // 2, g, sc, sh)
        _norm_rows(xn_ref, h_ref, HALO + tm, g, sc, sh, keep=ti != tps - 1)
        up(0)

    for parity in range(2):
        @pl.when((j >= 1) & (j < nf) & (j % 2 == parity))
        def _():
            activate(1 - parity, j - 1)
            up(parity)

    @pl.when(j == nf)
    def _():
        activate((nf - 1) % 2, nf - 1)
        down()

    @pl.when(j > nf)
    def _():
        down()


def _ffn(x2, layer, g, mods, w_up_t, conv_p, w_down_t, seq):
    rows, d = x2.shape
    tm = TM_FFN
    tps = seq // tm
    nf = D_FF // TF
    tn = TN_DOWN
    nd = d // tn
    hb = tm // HALO
    nhalo = rows // HALO
    last_half = 2 * (rows // tm)

    def first(j):
        return jnp.minimum(j, nf - 1)

    def second(j):
        return jnp.maximum(j - nf, 0)

    def lagged(j):
        return jnp.clip(j - 1, 0, nf - 1)

    return pl.pallas_call(
        functools.partial(_ffn_kernel, tps=tps, nf=nf),
        out_shape=jax.ShapeDtypeStruct((rows, d), F32),
        grid=(rows // tm, nf + nd),
        in_specs=[
            pl.BlockSpec((tm // 2, d), lambda i, j: (jnp.minimum(2 * (i + jnp.minimum(j, 1)), last_half - 2), 0)),
            pl.BlockSpec((tm // 2, d), lambda i, j: (2 * i + 1, 0), pipeline_mode=pl.Buffered(1)),
            pl.BlockSpec((HALO, d), lambda i, j: (jnp.maximum(i * hb - 1, 0), 0)),
            pl.BlockSpec((HALO, d), lambda i, j: (jnp.minimum((i + 1) * hb, nhalo - 1), 0)),
            pl.BlockSpec((1, 1, d), lambda i, j: (layer, 0, 0)),
            pl.BlockSpec((1, 1, d), lambda i, j: ((i // tps) * 6 + 4, 0, 0)),
            pl.BlockSpec((1, 1, d), lambda i, j: ((i // tps) * 6 + 3, 0, 0)),
            pl.BlockSpec((1, 1, tn), lambda i, j: ((i // tps) * 6 + 5, 0, second(j))),
            pl.BlockSpec((tm, tn), lambda i, j: (i, second(j))),
            pl.BlockSpec((1, d, TF), lambda i, j: (layer, 0, first(j))),
            pl.BlockSpec((1, d, TF), lambda i, j: (layer, 0, nf + first(j))),
            pl.BlockSpec((1, 8, TF), lambda i, j: (layer, 0, lagged(j))),
            pl.BlockSpec((1, 8, TF), lambda i, j: (layer, 0, nf + lagged(j))),
            pl.BlockSpec((1, D_FF, tn), lambda i, j: (layer, 0, second(j))),
        ],
        out_specs=pl.BlockSpec((tm, tn), lambda i, j: (i, second(j))),
        scratch_shapes=[pltpu.VMEM((tm + 2 * HALO, d), BF16),
                        pltpu.VMEM((tm + 2 * HALO, TF), F32),
                        pltpu.VMEM((tm + 2 * HALO, TF), F32),
                        pltpu.VMEM((tm + 2 * HALO, TF), F32),
                        pltpu.VMEM((tm + 2 * HALO, TF), F32),
                        pltpu.VMEM((nf, tm, TF), BF16)],
        compiler_params=_params(("arbitrary", "arbitrary"), VMEM_LIMIT_FFN_BYTES),
        name="conv_ffn",
    )(x2, x2, x2, x2, g, mods, mods, mods, x2, w_up_t, w_up_t, conv_p, conv_p, w_down_t)


def _final_kernel(x_ref, g_ref, o_ref):
    x = x_ref[...]
    ms = jnp.mean(x * x, axis=-1, keepdims=True)
    o_ref[...] = x * lax.rsqrt(ms + EPS) * g_ref[...]


def _final_norm(x2, g_final):
    rows, d = x2.shape
    tr = 256
    return pl.pallas_call(
        _final_kernel,
        out_shape=jax.ShapeDtypeStruct((rows, d), F32),
        grid=(rows // tr,),
        in_specs=[pl.BlockSpec((tr, d), lambda i: (i, 0)), pl.BlockSpec((1, d), lambda i: (0, 0))],
        out_specs=pl.BlockSpec((tr, d), lambda i: (i, 0)),
        compiler_params=_params(("arbitrary",)),
        name="final_norm",
    )(x2, g_final)


def _rope_tables(seq):
    half = RET_HEAD_DIM // 2
    inv = ROPE_BASE ** (-jnp.arange(half, dtype=F32) / half)
    ang = jnp.arange(seq, dtype=F32)[:, None] * inv[None, :]
    cos = jnp.cos(ang)
    sin = jnp.sin(ang)
    return jnp.concatenate([cos, cos], axis=1), jnp.concatenate([-sin, sin], axis=1)


def _trunk(x, mods, wts):
    batch, seq, d = x.shape
    x2 = x.reshape(batch * seq, d)
    cos_t, sin_t = _rope_tables(seq)
    fc = _fourier_consts(seq)
    for l in range(DEPTH):
        m = mods[l]
        pool_in, four_in, ret_in, na_in, h = _in_proj(x2, l, wts["g_mix"], m, cos_t, sin_t, wts["w_in"], seq)
        branches = (
            _pool_branch(pool_in, l, wts["pool_w"], wts["pool_scale"], seq),
            _fourier_branch(four_in, fc, batch, seq),
            _ret_branch(ret_in, l, wts["ret_decay_f"], wts["ret_decay_b"], batch, seq),
            _na_branch(na_in, l, wts["na_bias"], batch, seq),
        )
        x2 = _merge(x2, h, l, m, branches, wts["w_gate"], wts["b_gate"], wts["w_branch"], wts["w_out"], seq)
        x2 = _ffn(x2, l, wts["g_ffn"], m, wts["w_up"], wts["conv_p"], wts["w_down"], seq)
    return _final_norm(x2, wts["g_final"]).reshape(batch, seq, d)


def kernel(x_prompt, x_sample, c_prompt, c_sample, w_ada, b_ada, g_mix, w_in, pool_w, pool_scale,
           ret_decay_f, ret_decay_b, na_rpb, w_branch, w_gate, b_gate, w_out, g_ffn, w_up, conv_w,
           conv_b, w_down, g_final):
    d = D_MODEL
    nb_p = c_prompt.shape[0]
    nb_s = c_sample.shape[0]
    c8 = jnp.concatenate([c_prompt, c_sample, jnp.zeros((8 - nb_p - nb_s, d), F32)], axis=0)
    mod_all = _ada(c8, w_ada, b_ada)
    mods_p = mod_all[:, :nb_p].reshape(DEPTH, nb_p * 6, 1, d)
    mods_s = mod_all[:, nb_p:nb_p + nb_s].reshape(DEPTH, nb_s * 6, 1, d)
    decay_shape = (DEPTH, RET_HEADS, 1, RET_HEAD_DIM)
    wts = dict(
        g_mix=g_mix.reshape(DEPTH, 1, d),
        w_in=w_in.astype(BF16),
        pool_w=pool_w.astype(BF16),
        pool_scale=pool_scale.reshape(DEPTH, 1, BRANCH_WIDTH),
        ret_decay_f=jnp.broadcast_to(ret_decay_f.astype(F32)[:, :, None, None], decay_shape),
        ret_decay_b=jnp.broadcast_to(ret_decay_b.astype(F32)[:, :, None, None], decay_shape),
        na_bias=_na_bias(na_rpb),
        w_branch=w_branch.astype(BF16),
        w_gate=w_gate.astype(BF16),
        b_gate=b_gate.reshape(DEPTH, 1, N_BRANCH * d),
        w_out=w_out.astype(BF16),
        g_ffn=g_ffn.reshape(DEPTH, 1, d),
        w_up=w_up.astype(BF16),
        conv_p=jnp.concatenate([conv_w, conv_b[:, None, :], jnp.zeros((DEPTH, 4, 2 * D_FF), F32)], axis=1),
        w_down=w_down.astype(BF16),
        g_final=g_final.reshape(1, d),
    )
    y_prompt = _trunk(x_prompt, mods_p, wts)
    y_sample = _trunk(x_sample, mods_s, wts)
    return (y_prompt, y_sample)
```
